```python
import math
import jax
import jax.numpy as jnp
from jax import lax
import numpy as np

D_MODEL = 4096
BATCH = 32
SEQ = 256
DEPTH = 4
DEC_BATCH = 2
DEC_SEQ = 4096
PAST_LEN = 512

GRID_W = 64
Q_BLOCK = 128
N_MIXERS = 3
ROPE_THETA = 10000.0
EPS = 1e-6
N_MOD = 6

GQA_HEADS = 32
GQA_KV_HEADS = 8
GQA_HEAD_DIM = D_MODEL // GQA_HEADS
DIFF_HEADS = 16
DIFF_HEAD_DIM = D_MODEL // DIFF_HEADS // 2
MLA_HEADS = 32
MLA_Q_RANK = D_MODEL // 4
MLA_KV_RANK = D_MODEL // 8
MLA_NOPE_DIM = 128
MLA_ROPE_DIM = 64
MLA_QK_DIM = MLA_NOPE_DIM + MLA_ROPE_DIM
MLA_V_DIM = 128
FFN_HIDDEN = ((8 * D_MODEL + 3 * 256 - 1) // (3 * 256)) * 256

kernel_name = "hybrid_dit_prefix_step"


def rmsnorm(x, g):
    xf = x.astype(jnp.float32)
    y = xf * lax.rsqrt(jnp.mean(xf * xf, axis=-1, keepdims=True) + EPS)
    return (y * g.astype(jnp.float32)).astype(x.dtype)


def modulation(cvec, w, b):
    m = jax.nn.silu(cvec) @ w + b
    return jnp.split(m, N_MOD, axis=-1)


def modulate(x, g, shift, scale):
    return rmsnorm(x, g) * (1 + scale) + shift


def swiglu(h, w13, w2):
    gt, up = jnp.split(h @ w13, 2, axis=-1)
    return (jax.nn.silu(gt) * up) @ w2


def axial_rope_tables(n_tokens, rot_dim):
    rows = n_tokens // GRID_W
    row_pos = jnp.repeat(jnp.arange(rows, dtype=jnp.float32), GRID_W)
    col_pos = jnp.tile(jnp.arange(GRID_W, dtype=jnp.float32), rows)
    half = rot_dim // 2
    inv = ROPE_THETA ** (-jnp.arange(0, half, 2, dtype=jnp.float32) / half)
    ang_r = row_pos[:, None] * inv
    ang_c = col_pos[:, None] * inv
    return (jnp.cos(ang_r), jnp.sin(ang_r), jnp.cos(ang_c), jnp.sin(ang_c))


def _rotate_half(x, cos, sin):
    x1, x2 = jnp.split(x, 2, axis=-1)
    return jnp.concatenate([x1 * cos - x2 * sin, x2 * cos + x1 * sin], axis=-1)


def apply_axial_rope(x, rope):
    cos_r, sin_r, cos_c, sin_c = rope
    xr, xc = jnp.split(x.astype(jnp.float32), 2, axis=-1)
    bc = lambda a: a[None, :, None, :]
    out = jnp.concatenate([_rotate_half(xr, bc(cos_r), bc(sin_r)),
                           _rotate_half(xc, bc(cos_c), bc(sin_c))], axis=-1)
    return out.astype(x.dtype)


def sweep_query_blocks(fn, q):
    B, S = q.shape[:2]
    nb = S // Q_BLOCK
    qb = jnp.moveaxis(q.reshape((B, nb, Q_BLOCK) + q.shape[2:]), 1, 0)
    o = lax.map(fn, qb)
    return jnp.moveaxis(o, 0, 1).reshape((B, S) + o.shape[3:])


def softmax_attention(q, k, v):
    B, S, H, dk = q.shape
    G = k.shape[2]
    R = H // G
    scale = dk ** -0.5
    qg = q.reshape(B, S, G, R, dk)

    def block(qb):
        s = jnp.einsum("bqgrd,btgd->bgrqt", qb, k, preferred_element_type=jnp.float32) * scale
        p = jax.nn.softmax(s, axis=-1).astype(v.dtype)
        return jnp.einsum("bgrqt,btge->bqgre", p, v)

    o = sweep_query_blocks(block, qg)
    return o.reshape(B, S, H, v.shape[-1])


def differential_attention(q, k, v, lam):
    scale = q.shape[-1] ** -0.5

    def block(qb):
        s = jnp.einsum("bqhcd,bthcd->bhcqt", qb, k, preferred_element_type=jnp.float32) * scale
        p = jax.nn.softmax(s, axis=-1)
        w = (p[:, :, 0] - lam * p[:, :, 1]).astype(v.dtype)
        return jnp.einsum("bhqt,bthe->bqhe", w, v)

    return sweep_query_blocks(block, q)


def gqa_qkv(h, p):
    wqkv, qg, kg, _ = p
    B, S, _ = h.shape
    q, k, v = jnp.split(h @ wqkv, [GQA_HEADS * GQA_HEAD_DIM,
                                   (GQA_HEADS + GQA_KV_HEADS) * GQA_HEAD_DIM], axis=-1)
    q = rmsnorm(q.reshape(B, S, GQA_HEADS, GQA_HEAD_DIM), qg)
    k = rmsnorm(k.reshape(B, S, GQA_KV_HEADS, GQA_HEAD_DIM), kg)
    v = v.reshape(B, S, GQA_KV_HEADS, GQA_HEAD_DIM)
    return q, k, v


def gqa_context(h, p):
    B, S, _ = h.shape
    q, k, v = gqa_qkv(h, p)
    o = softmax_attention(q, k, v)
    return o.reshape(B, S, -1) @ p[3], (k, v)


def gqa_latent(h, p, cache, rope):
    B, S, _ = h.shape
    q, k, v = gqa_qkv(h, p)
    q = apply_axial_rope(q, rope)
    k = apply_axial_rope(k, rope)
    k_all = jnp.concatenate([cache[0], k], axis=1)
    v_all = jnp.concatenate([cache[1], v], axis=1)
    o = softmax_attention(q, k_all, v_all)
    return o.reshape(B, S, -1) @ p[3]


def diff_lambda_init(layer_idx):
    return 0.8 - 0.6 * math.exp(-0.3 * layer_idx)


def diff_lambda(p, lam_init):
    lq1, lk1, lq2, lk2 = [a.astype(jnp.float32) for a in p[3:7]]
    return jnp.exp(jnp.sum(lq1 * lk1)) - jnp.exp(jnp.sum(lq2 * lk2)) + lam_init


def diff_qkv(h, p):
    B, S, _ = h.shape
    q, k, v = jnp.split(h @ p[0], 3, axis=-1)
    q = rmsnorm(q.reshape(B, S, DIFF_HEADS, 2, DIFF_HEAD_DIM), p[1])
    k = rmsnorm(k.reshape(B, S, DIFF_HEADS, 2, DIFF_HEAD_DIM), p[2])
    v = v.reshape(B, S, DIFF_HEADS, 2 * DIFF_HEAD_DIM)
    return q, k, v


def rope_pairs(x, rope):
    B, S, H, _, d = x.shape
    return apply_axial_rope(x.reshape(B, S, 2 * H, d), rope).reshape(x.shape)


def diff_output(o, p, lam_init):
    B, S = o.shape[:2]
    o = rmsnorm(o, p[7]) * (1.0 - lam_init)
    return o.reshape(B, S, -1) @ p[8]


def diff_context(h, p, lam_init):
    q, k, v = diff_qkv(h, p)
    o = differential_attention(q, k, v, diff_lambda(p, lam_init))
    return diff_output(o, p, lam_init), (k, v)


def diff_latent(h, p, cache, rope, lam_init):
    q, k, v = diff_qkv(h, p)
    q = rope_pairs(q, rope)
    k = rope_pairs(k, rope)
    k_all = jnp.concatenate([cache[0], k], axis=1)
    v_all = jnp.concatenate([cache[1], v], axis=1)
    o = differential_attention(q, k_all, v_all, diff_lambda(p, lam_init))
    return diff_output(o, p, lam_init)


def mla_compress(h, p):
    wdown, qa_g, _, kva_g = p[:4]
    cq, ckv, kpe = jnp.split(h @ wdown, [MLA_Q_RANK, MLA_Q_RANK + MLA_KV_RANK], axis=-1)
    return rmsnorm(cq, qa_g), rmsnorm(ckv, kva_g), kpe


def mla_queries(cq, p):
    B, S, _ = cq.shape
    q = (cq @ p[2]).reshape(B, S, MLA_HEADS, MLA_QK_DIM)
    return rmsnorm(q, p[5])


def mla_keys_values(ckv, kpe, p):
    B, T, _ = ckv.shape
    kv = (ckv @ p[4]).reshape(B, T, MLA_HEADS, MLA_NOPE_DIM + MLA_V_DIM)
    k_nope, v = jnp.split(kv, [MLA_NOPE_DIM], axis=-1)
    k_pe = jnp.broadcast_to(kpe[:, :, None, :], (B, T, MLA_HEADS, MLA_ROPE_DIM))
    k = rmsnorm(jnp.concatenate([k_nope, k_pe], axis=-1), p[6])
    return k, v


def rope_tail(x, rope):
    return jnp.concatenate([x[..., :MLA_NOPE_DIM],
                            apply_axial_rope(x[..., MLA_NOPE_DIM:], rope)], axis=-1)


def mla_context(h, p):
    B, S, _ = h.shape
    cq, ckv, kpe = mla_compress(h, p)
    q = mla_queries(cq, p)
    k, v = mla_keys_values(ckv, kpe, p)
    o = softmax_attention(q, k, v)
    return o.reshape(B, S, -1) @ p[7], (ckv, kpe)


def mla_latent(h, p, cache, rope):
    B, S, _ = h.shape
    cq, ckv, kpe = mla_compress(h, p)
    q = rope_tail(mla_queries(cq, p), rope)
    k, v = mla_keys_values(ckv, kpe, p)
    k = rope_tail(k, rope)
    k_ctx, v_ctx = mla_keys_values(cache[0], cache[1], p)
    o = softmax_attention(q, jnp.concatenate([k_ctx, k], axis=1),
                          jnp.concatenate([v_ctx, v], axis=1))
    return o.reshape(B, S, -1) @ p[7]


def setup_inputs(seed: int = 0) -> dict:
    key = jax.random.key(seed)
    ks = jax.random.split(key, 128)
    counter = [0]
    D = D_MODEL

    def nrm(shape, scale=1.0):
        k = ks[counter[0]]
        counter[0] += 1
        return jax.random.normal(k, shape, jnp.float32) * scale

    def gain(n):
        return 1.0 + nrm((n,), 0.02)

    inp = {}
    inp["x_prompt"] = nrm((BATCH, SEQ, D))
    inp["x_sample"] = nrm((DEC_BATCH, DEC_SEQ, D))
    inp["c"] = nrm((DEC_BATCH, D))
    for l in range(DEPTH):
        kind = l % N_MIXERS
        if kind == 0:
            inp[f"cache_k{l}"] = nrm((DEC_BATCH, PAST_LEN, GQA_KV_HEADS, GQA_HEAD_DIM))
            inp[f"cache_v{l}"] = nrm((DEC_BATCH, PAST_LEN, GQA_KV_HEADS, GQA_HEAD_DIM))
        elif kind == 1:
            inp[f"cache_k{l}"] = nrm((DEC_BATCH, PAST_LEN, DIFF_HEADS, 2, DIFF_HEAD_DIM))
            inp[f"cache_v{l}"] = nrm((DEC_BATCH, PAST_LEN, DIFF_HEADS, 2 * DIFF_HEAD_DIM))
        else:
            inp[f"cache_ckv{l}"] = nrm((DEC_BATCH, PAST_LEN, MLA_KV_RANK))
            inp[f"cache_kpe{l}"] = nrm((DEC_BATCH, PAST_LEN, MLA_ROPE_DIM))
    inp["c_ctx"] = nrm((D,))
    for l in range(DEPTH):
        kind = l % N_MIXERS
        inp[f"ada_w{l}"] = nrm((D, N_MOD * D), D ** -0.5)
        inp[f"ada_b{l}"] = nrm((N_MOD * D,), 0.02)
        inp[f"norm1_g{l}"] = gain(D)
        inp[f"norm2_g{l}"] = gain(D)
        inp[f"ffn_w13_{l}"] = nrm((D, 2 * FFN_HIDDEN), D ** -0.5)
        inp[f"ffn_w2_{l}"] = nrm((FFN_HIDDEN, D), FFN_HIDDEN ** -0.5)
        if kind == 0:
            inp[f"gqa_wqkv{l}"] = nrm((D, (GQA_HEADS + 2 * GQA_KV_HEADS) * GQA_HEAD_DIM), D ** -0.5)
            inp[f"gqa_qg{l}"] = gain(GQA_HEAD_DIM)
            inp[f"gqa_kg{l}"] = gain(GQA_HEAD_DIM)
            inp[f"gqa_wo{l}"] = nrm((GQA_HEADS * GQA_HEAD_DIM, D), (GQA_HEADS * GQA_HEAD_DIM) ** -0.5)
        elif kind == 1:
            inp[f"diff_wqkv{l}"] = nrm((D, 3 * D), D ** -0.5)
            inp[f"diff_qg{l}"] = gain(DIFF_HEAD_DIM)
            inp[f"diff_kg{l}"] = gain(DIFF_HEAD_DIM)
            inp[f"diff_lq1_{l}"] = nrm((DIFF_HEAD_DIM,), 0.1)
            inp[f"diff_lk1_{l}"] = nrm((DIFF_HEAD_DIM,), 0.1)
            inp[f"diff_lq2_{l}"] = nrm((DIFF_HEAD_DIM,), 0.1)
            inp[f"diff_lk2_{l}"] = nrm((DIFF_HEAD_DIM,), 0.1)
            inp[f"diff_subln_g{l}"] = gain(2 * DIFF_HEAD_DIM)
            inp[f"diff_wo{l}"] = nrm((D, D), D ** -0.5)
        else:
            inp[f"mla_wdown{l}"] = nrm((D, MLA_Q_RANK + MLA_KV_RANK + MLA_ROPE_DIM), D ** -0.5)
            inp[f"mla_qa_g{l}"] = gain(MLA_Q_RANK)
            inp[f"mla_wuq{l}"] = nrm((MLA_Q_RANK, MLA_HEADS * MLA_QK_DIM), MLA_Q_RANK ** -0.5)
            inp[f"mla_kva_g{l}"] = gain(MLA_KV_RANK)
            inp[f"mla_wukv{l}"] = nrm((MLA_KV_RANK, MLA_HEADS * (MLA_NOPE_DIM + MLA_V_DIM)), MLA_KV_RANK ** -0.5)
            inp[f"mla_qg{l}"] = gain(MLA_QK_DIM)
            inp[f"mla_kg{l}"] = gain(MLA_QK_DIM)
            inp[f"mla_wo{l}"] = nrm((MLA_HEADS * MLA_V_DIM, D), (MLA_HEADS * MLA_V_DIM) ** -0.5)
    return inp


def reference(x_prompt, x_sample, c,
              cache_k0, cache_v0, cache_k1, cache_v1, cache_ckv2, cache_kpe2, cache_k3, cache_v3,
              c_ctx,
              ada_w0, ada_b0, norm1_g0, norm2_g0, ffn_w13_0, ffn_w2_0,
              gqa_wqkv0, gqa_qg0, gqa_kg0, gqa_wo0,
              ada_w1, ada_b1, norm1_g1, norm2_g1, ffn_w13_1, ffn_w2_1,
              diff_wqkv1, diff_qg1, diff_kg1, diff_lq1_1, diff_lk1_1, diff_lq2_1, diff_lk2_1,
              diff_subln_g1, diff_wo1,
              ada_w2, ada_b2, norm1_g2, norm2_g2, ffn_w13_2, ffn_w2_2,
              mla_wdown2, mla_qa_g2, mla_wuq2, mla_kva_g2, mla_wukv2, mla_qg2, mla_kg2, mla_wo2,
              ada_w3, ada_b3, norm1_g3, norm2_g3, ffn_w13_3, ffn_w2_3,
              gqa_wqkv3, gqa_qg3, gqa_kg3, gqa_wo3):
    common = [
        (ada_w0, ada_b0, norm1_g0, norm2_g0, ffn_w13_0, ffn_w2_0),
        (ada_w1, ada_b1, norm1_g1, norm2_g1, ffn_w13_1, ffn_w2_1),
        (ada_w2, ada_b2, norm1_g2, norm2_g2, ffn_w13_2, ffn_w2_2),
        (ada_w3, ada_b3, norm1_g3, norm2_g3, ffn_w13_3, ffn_w2_3),
    ]
    mixers = [
        (gqa_wqkv0, gqa_qg0, gqa_kg0, gqa_wo0),
        (diff_wqkv1, diff_qg1, diff_kg1, diff_lq1_1, diff_lk1_1, diff_lq2_1, diff_lk2_1,
         diff_subln_g1, diff_wo1),
        (mla_wdown2, mla_qa_g2, mla_wuq2, mla_kva_g2, mla_wukv2, mla_qg2, mla_kg2, mla_wo2),
        (gqa_wqkv3, gqa_qg3, gqa_kg3, gqa_wo3),
    ]
    caches = [(cache_k0, cache_v0), (cache_k1, cache_v1),
              (cache_ckv2, cache_kpe2), (cache_k3, cache_v3)]

    n_lat = x_sample.shape[1]
    rope_gqa = axial_rope_tables(n_lat, GQA_HEAD_DIM)
    rope_diff = axial_rope_tables(n_lat, DIFF_HEAD_DIM)
    rope_mla = axial_rope_tables(n_lat, MLA_ROPE_DIM)

    xc, xl = x_prompt, x_sample
    new_state = []
    for l in range(DEPTH):
        ada_w, ada_b, g1, g2, w13, w2 = common[l]
        mp, cache = mixers[l], caches[l]
        c_sh1, c_sc1, c_gt1, c_sh2, c_sc2, c_gt2 = modulation(c_ctx[None, None, :], ada_w, ada_b)
        l_sh1, l_sc1, l_gt1, l_sh2, l_sc2, l_gt2 = modulation(c[:, None, :], ada_w, ada_b)
        hc = modulate(xc, g1, c_sh1, c_sc1)
        hl = modulate(xl, g1, l_sh1, l_sc1)
        kind = l % N_MIXERS
        if kind == 0:
            oc, st = gqa_context(hc, mp)
            ol = gqa_latent(hl, mp, cache, rope_gqa)
        elif kind == 1:
            lam_init = diff_lambda_init(l)
            oc, st = diff_context(hc, mp, lam_init)
            ol = diff_latent(hl, mp, cache, rope_diff, lam_init)
        else:
            oc, st = mla_context(hc, mp)
            ol = mla_latent(hl, mp, cache, rope_mla)
        new_state.extend(st)
        xc = xc + c_gt1 * oc
        xl = xl + l_gt1 * ol
        xc = xc + c_gt2 * swiglu(modulate(xc, g2, c_sh2, c_sc2), w13, w2)
        xl = xl + l_gt2 * swiglu(modulate(xl, g2, l_sh2, l_sc2), w13, w2)

    y_prompt, y_sample = xc, xl
    return (y_prompt, y_sample, *new_state)
```

```python
import functools
import math

import jax
import jax.numpy as jnp
from jax import lax
from jax.experimental import pallas as pl
from jax.experimental.pallas import tpu as pltpu

F32 = jnp.float32
BF16 = jnp.bfloat16

EPS = 1e-6
ROPE_THETA = 10000.0
GRID_W = 64
N_MOD = 6
N_MIXERS = 3
LANES = 128
SUBLANES = 8
VMEM_LIMIT_BYTES = 56 * 1024 * 1024
NEG_BIG = -1e30

ROW_TILE = 1024
COL_TILE = 512
SWIGLU_TILE = 256
NORM_ROWS = 256
MOD_COLS = 1024
KV_CHUNK = 512
Q_ROWS_GROUPED = 256
Q_ROWS_SINGLE = 512


def _params(*sem):
    return pltpu.CompilerParams(dimension_semantics=sem, vmem_limit_bytes=VMEM_LIMIT_BYTES)


def _tile(target, n, *also):
    t = min(target, n)
    while any(d % t for d in (n,) + also):
        t //= 2
    assert t >= 1
    return t


class _Geom:
    def __init__(self, n_ctx, n_req, dec_seq):
        self.n_ctx, self.n_req, self.dec_seq = n_ctx, n_req, dec_seq
        self.m = n_ctx + n_req * dec_seq

    def row_tile(self, target):
        return _tile(target, self.n_ctx, self.dec_seq)

    def group_of_block(self, i, bm):
        nc, per = self.n_ctx // bm, self.dec_seq // bm
        return jnp.where(i < nc, 0, 1 + (i - nc) // per)

    def rope_block(self, i, bm):
        nc, per = self.n_ctx // bm, self.dec_seq // bm
        return jnp.where(i < nc, 0, 1 + (i - nc) % per)


def _rope_cos_sin(n_tokens, rot_dim):
    rows = n_tokens // GRID_W
    row_pos = jnp.repeat(jnp.arange(rows, dtype=F32), GRID_W)
    col_pos = jnp.tile(jnp.arange(GRID_W, dtype=F32), rows)
    half = rot_dim // 2
    inv = ROPE_THETA ** (-jnp.arange(0, half, 2, dtype=F32) / half)
    ang_r = row_pos[:, None] * inv
    ang_c = col_pos[:, None] * inv
    cos = jnp.concatenate([jnp.cos(ang_r), jnp.cos(ang_r), jnp.cos(ang_c), jnp.cos(ang_c)], axis=-1)
    sin = jnp.concatenate([-jnp.sin(ang_r), jnp.sin(ang_r), -jnp.sin(ang_c), jnp.sin(ang_c)], axis=-1)
    return cos, sin


def _rope_table(n_tokens, rot_dim, width, lane0, bm):
    cos, sin = _rope_cos_sin(n_tokens, rot_dim)
    cos_full = jnp.ones((n_tokens, width), F32).at[:, lane0:lane0 + rot_dim].set(cos)
    sin_full = jnp.zeros((n_tokens, width), F32).at[:, lane0:lane0 + rot_dim].set(sin)
    cos_tab = jnp.concatenate([jnp.ones((bm, width), F32), cos_full], axis=0)
    sin_tab = jnp.concatenate([jnp.zeros((bm, width), F32), sin_full], axis=0)
    return cos_tab, sin_tab


def _rope_slab(y, cos, sin, rb):
    lane = lax.broadcasted_iota(jnp.int32, y.shape, 1)
    first = ((lane // rb) % 2) == 0
    partner = jnp.where(first, pltpu.roll(y, LANES - rb, 1), pltpu.roll(y, rb, 1))
    return y * cos + partner * sin


def _mod_kernel(c_ref, w_ref, b_ref, o_ref):
    a = jax.nn.silu(c_ref[...]).astype(BF16)
    o_ref[...] = jnp.dot(a, w_ref[...].astype(BF16), preferred_element_type=F32) + b_ref[...]


def _modulation(cs, w, b):
    d, n = w.shape
    bn = _tile(MOD_COLS, n)
    return pl.pallas_call(
        _mod_kernel,
        grid=(n // bn,),
        in_specs=[pl.BlockSpec((SUBLANES, d), lambda j: (0, 0)),
                  pl.BlockSpec((d, bn), lambda j: (0, j)),
                  pl.BlockSpec((1, bn), lambda j: (0, j))],
        out_specs=pl.BlockSpec((SUBLANES, bn), lambda j: (0, j)),
        out_shape=jax.ShapeDtypeStruct((SUBLANES, n), F32),
        compiler_params=_params("arbitrary"),
        name="adaln_modulation",
    )(cs, w, b.reshape(1, n))


def _norm_kernel(x_ref, g_ref, sh_ref, sc_ref, o_ref):
    x = x_ref[...]
    ms = jnp.mean(x * x, axis=-1, keepdims=True)
    y = x * lax.rsqrt(ms + EPS) * g_ref[...]
    o_ref[...] = (y * (1.0 + sc_ref[...]) + sh_ref[...]).astype(o_ref.dtype)


def _norm_modulate(geom, x, col_block, width, g, mod3, shift_kind, scale_kind, out_dtype):
    m = x.shape[0]
    bm = geom.row_tile(NORM_ROWS)

    def mod_spec(kind):
        return pl.BlockSpec((None, 1, width), lambda i: (geom.group_of_block(i, bm) * N_MOD + kind, 0, 0))

    return pl.pallas_call(
        _norm_kernel,
        grid=(m // bm,),
        in_specs=[pl.BlockSpec((bm, width), lambda i: (i, col_block)),
                  pl.BlockSpec((1, width), lambda i: (0, 0)),
                  mod_spec(shift_kind), mod_spec(scale_kind)],
        out_specs=pl.BlockSpec((bm, width), lambda i: (i, 0)),
        out_shape=jax.ShapeDtypeStruct((m, width), out_dtype),
        compiler_params=_params("arbitrary"),
        name="rmsnorm_modulate",
    )(x, g.reshape(1, width), mod3, mod3)


def _plain_norm(geom, x, col_block, width, g, out_dtype):
    zeros = jnp.zeros((N_MOD * (1 + geom.n_req), 1, width), F32)
    return _norm_modulate(geom, x, col_block, width, g, zeros, 0, 1, out_dtype)


def _matmul_call(a, weights, extras, extra_specs, out_shapes, out_specs, epilogue, *, bm, bn, n_cols, name):
    m, k = a.shape
    n_w, n_ex, n_out = len(weights), len(extras), len(out_shapes)
    needs_cast = [w.dtype != BF16 for w, _ in weights]

    def kernel(*refs):
        a_ref = refs[0]
        w_refs = refs[1:1 + n_w]
        ex_refs = refs[1 + n_w:1 + n_w + n_ex]
        out_refs = refs[1 + n_w + n_ex:1 + n_w + n_ex + n_out]
        scratch = list(refs[1 + n_w + n_ex + n_out:])
        accs = []
        for w_ref, cast in zip(w_refs, needs_cast):
            if cast:
                w_bf16 = scratch.pop(0)

                @pl.when(pl.program_id(1) == 0)
                def _(w_ref=w_ref, w_bf16=w_bf16):
                    w_bf16[...] = w_ref[...].astype(BF16)

                w_val = w_bf16[...]
            else:
                w_val = w_ref[...]
            accs.append(jnp.dot(a_ref[...], w_val, preferred_element_type=F32))
        epilogue(accs, ex_refs, out_refs)

    in_specs = [pl.BlockSpec((bm, k), lambda j, i: (i, 0))]
    for _, off in weights:
        in_specs.append(pl.BlockSpec((k, bn), lambda j, i, off=off: (0, off + j)))
    in_specs += list(extra_specs)
    return pl.pallas_call(
        kernel,
        grid=(n_cols // bn, m // bm),
        in_specs=in_specs,
        out_specs=out_specs,
        out_shape=out_shapes,
        scratch_shapes=[pltpu.VMEM((k, bn), BF16) for c in needs_cast if c],
        compiler_params=_params("arbitrary", "arbitrary"),
        name=name,
    )(a, *[w for w, _ in weights], *extras)


def _matmul_plain(a, w, out_dtype):
    m, n = a.shape[0], w.shape[1]
    bm, bn = _tile(ROW_TILE, m), _tile(COL_TILE, n)

    def epilogue(accs, ex, outs):
        outs[0][...] = accs[0].astype(out_dtype)

    return _matmul_call(a, [(w, 0)], [], [], [jax.ShapeDtypeStruct((m, n), out_dtype)],
                        [pl.BlockSpec((bm, bn), lambda j, i: (i, j))], epilogue,
                        bm=bm, bn=bn, n_cols=n, name="matmul")[0]


def _matmul_gated_residual(geom, a, w, res, mod3, gate_kind, row_target=ROW_TILE):
    m, n = a.shape[0], w.shape[1]
    bm, bn = geom.row_tile(row_target), _tile(COL_TILE, n)

    def epilogue(accs, ex, outs):
        res_ref, gate_ref = ex
        outs[0][...] = res_ref[...] + gate_ref[...] * accs[0]

    extra_specs = [pl.BlockSpec((bm, bn), lambda j, i: (i, j)),
                   pl.BlockSpec((None, 1, bn), lambda j, i: (geom.group_of_block(i, bm) * N_MOD + gate_kind, 0, j))]
    return _matmul_call(a, [(w, 0)], [res, mod3], extra_specs, [jax.ShapeDtypeStruct((m, n), F32)],
                        [pl.BlockSpec((bm, bn), lambda j, i: (i, j))], epilogue,
                        bm=bm, bn=bn, n_cols=n, name="matmul_gated_residual")[0]


def _matmul_swiglu(a, w13):
    m, f = a.shape[0], w13.shape[1] // 2
    bm, bn = _tile(ROW_TILE, m), _tile(SWIGLU_TILE, f)

    def epilogue(accs, ex, outs):
        outs[0][...] = (jax.nn.silu(accs[0]) * accs[1]).astype(BF16)

    return _matmul_call(a, [(w13, 0), (w13, f // bn)], [], [], [jax.ShapeDtypeStruct((m, f), BF16)],
                        [pl.BlockSpec((bm, bn), lambda j, i: (i, j))], epilogue,
                        bm=bm, bn=bn, n_cols=f, name="matmul_swiglu")[0]


def _matmul_heads(geom, a, w, gain_cols, cos_tab, sin_tab, *, head_width, n_valid, n_norm_cols, rb,
                  rope_slabs, out_dtype, bm):
    m, n = a.shape[0], w.shape[1]
    bn = _tile(COL_TILE, n, n_norm_cols)
    assert bn % head_width == 0 and head_width % LANES == 0
    n_norm_blocks, n_blocks = n_norm_cols // bn, n // bn

    def epilogue(accs, ex, outs):
        acc = accs[0]
        gain_ref, cos_ref, sin_ref = ex
        o_ref = outs[0]

        def normed():
            for h in range(bn // head_width):
                lo = h * head_width
                y = acc[:, lo:lo + head_width]
                ms = jnp.sum(y * y, axis=-1, keepdims=True) / n_valid
                y = y * lax.rsqrt(ms + EPS) * gain_ref[:, lo:lo + head_width]
                for s in range(head_width // LANES):
                    ys = y[:, s * LANES:(s + 1) * LANES]
                    if s in rope_slabs:
                        ys = _rope_slab(ys, cos_ref[:, s * LANES:(s + 1) * LANES],
                                        sin_ref[:, s * LANES:(s + 1) * LANES], rb)
                    o_ref[:, lo + s * LANES:lo + (s + 1) * LANES] = ys.astype(out_dtype)

        if n_norm_blocks == n_blocks:
            normed()
        else:
            pl.when(pl.program_id(0) < n_norm_blocks)(normed)

            @pl.when(pl.program_id(0) >= n_norm_blocks)
            def _():
                o_ref[...] = acc.astype(out_dtype)

    extra_specs = [pl.BlockSpec((1, bn), lambda j, i: (0, j)),
                   pl.BlockSpec((bm, head_width), lambda j, i: (geom.rope_block(i, bm), 0)),
                   pl.BlockSpec((bm, head_width), lambda j, i: (geom.rope_block(i, bm), 0))]
    return _matmul_call(a, [(w, 0)], [gain_cols, cos_tab, sin_tab], extra_specs,
                        [jax.ShapeDtypeStruct((m, n), out_dtype)],
                        [pl.BlockSpec((bm, bn), lambda j, i: (i, j))], epilogue,
                        bm=bm, bn=bn, n_cols=n, name="matmul_head_norm_rope")[0]


def _matmul_mla_kv(ckv, w, kpe, cos_tab, sin_tab, kg_nope, kg_rope, rope_block, *, bm, rb, n_valid):
    m, n = ckv.shape[0], w.shape[1]
    hw = 2 * LANES
    bn = _tile(COL_TILE, n)
    assert bn % hw == 0

    def epilogue(accs, ex, outs):
        acc = accs[0]
        kpe_ref, cos_ref, sin_ref, gn_ref, gr_ref = ex
        k_ref, v_ref = outs
        pe = kpe_ref[...]
        ss_pe = jnp.sum(pe * pe, axis=-1, keepdims=True)
        pe_rot = _rope_slab(pe * gr_ref[...], cos_ref[...], sin_ref[...], rb)
        for h in range(bn // hw):
            kn = acc[:, h * hw:h * hw + LANES]
            ms = (jnp.sum(kn * kn, axis=-1, keepdims=True) + ss_pe) / n_valid
            r = lax.rsqrt(ms + EPS)
            k_ref[:, h * hw:h * hw + LANES] = (kn * r * gn_ref[...]).astype(BF16)
            k_ref[:, h * hw + LANES:(h + 1) * hw] = (pe_rot * r).astype(BF16)
            v_ref[:, h * LANES:(h + 1) * LANES] = acc[:, h * hw + LANES:(h + 1) * hw].astype(BF16)

    extra_specs = [pl.BlockSpec((bm, LANES), lambda j, i: (i, 0)),
                   pl.BlockSpec((bm, LANES), lambda j, i: (rope_block(i), 0)),
                   pl.BlockSpec((bm, LANES), lambda j, i: (rope_block(i), 0)),
                   pl.BlockSpec((1, LANES), lambda j, i: (0, 0)),
                   pl.BlockSpec((1, LANES), lambda j, i: (0, 0))]
    return _matmul_call(ckv, [(w, 0)], [kpe, cos_tab, sin_tab, kg_nope, kg_rope], extra_specs,
                        [jax.ShapeDtypeStruct((m, n), BF16), jax.ShapeDtypeStruct((m, n // 2), BF16)],
                        [pl.BlockSpec((bm, bn), lambda j, i: (i, j)),
                         pl.BlockSpec((bm, bn // 2), lambda j, i: (i, j))], epilogue,
                        bm=bm, bn=bn, n_cols=n, name="matmul_mla_kv")


def _online_softmax_step(q, k, v, m, l, acc):
    s = lax.dot_general(q, k, (((1,), (1,)), ((), ())), preferred_element_type=F32)
    m_new = jnp.maximum(m, jnp.max(s, axis=1, keepdims=True))
    alpha = jnp.exp(m - m_new)
    p = jnp.exp(s - m_new)
    l_new = alpha * l + jnp.sum(p, axis=1, keepdims=True)
    acc_new = alpha * acc + jnp.dot(p.astype(BF16), v, preferred_element_type=F32)
    return m_new, l_new, acc_new


def _softmax_start(rows, dv):
    return (jnp.full((rows, 1), NEG_BIG, F32), jnp.zeros((rows, 1), F32), jnp.zeros((rows, dv), F32))


def _attn_kernel(q_ref, k_ref, v_ref, o_ref, *, n_rep, dk, dv, scale, tk):
    bq, t = q_ref.shape[0], k_ref.shape[0]
    q = jnp.concatenate([q_ref[:, r * dk:(r + 1) * dk] for r in range(n_rep)], axis=0)
    q = (q.astype(F32) * scale).astype(BF16)

    def body(c, carry):
        off = pl.multiple_of(c * tk, tk)
        k = k_ref[pl.ds(off, tk), :].astype(BF16)
        v = v_ref[pl.ds(off, tk), :].astype(BF16)
        return _online_softmax_step(q, k, v, *carry)

    m, l, acc = lax.fori_loop(0, t // tk, body, _softmax_start(n_rep * bq, dv))
    o = acc / l
    for r in range(n_rep):
        o_ref[:, r * dv:(r + 1) * dv] = o[r * bq:(r + 1) * bq].astype(o_ref.dtype)


def _attention(q2d, k2d, v2d, *, n_batch, s, t, n_groups, n_rep, dk, dv, scale,
               q_row0=0, kv_row0=0, q_col0=0, k_col0=0, v_col0=0):
    bq = _tile(Q_ROWS_GROUPED if n_rep > 1 else Q_ROWS_SINGLE, s)
    tk = _tile(KV_CHUNK, t)
    qw, ow = n_rep * dk, n_rep * dv
    assert q_row0 % bq == 0 and kv_row0 % t == 0 and q_col0 % qw == 0 and k_col0 % dk == 0 and v_col0 % dv == 0
    qb0, kb0, qc0, kc0, vc0 = q_row0 // bq, kv_row0 // t, q_col0 // qw, k_col0 // dk, v_col0 // dv
    nq = s // bq
    kernel = functools.partial(_attn_kernel, n_rep=n_rep, dk=dk, dv=dv, scale=scale, tk=tk)
    return pl.pallas_call(
        kernel,
        grid=(n_batch, n_groups, nq),
        in_specs=[pl.BlockSpec((bq, qw), lambda b, g, i: (qb0 + b * nq + i, qc0 + g)),
                  pl.BlockSpec((t, dk), lambda b, g, i: (kb0 + b, kc0 + g)),
                  pl.BlockSpec((t, dv), lambda b, g, i: (kb0 + b, vc0 + g))],
        out_specs=pl.BlockSpec((bq, ow), lambda b, g, i: (b * nq + i, g)),
        out_shape=jax.ShapeDtypeStruct((n_batch * s, n_groups * ow), BF16),
        compiler_params=_params("arbitrary", "arbitrary", "arbitrary"),
        name="attention",
    )(q2d, k2d, v2d)


def _diff_attn_kernel(q_ref, k_ref, v_ref, lq1_ref, lk1_ref, lq2_ref, lk2_ref, g_ref, o_ref, *,
                      d, scale, tk, lam_init):
    bq, t = q_ref.shape[0], k_ref.shape[0]
    dv = v_ref.shape[1]
    q = (q_ref[...].astype(F32) * scale).astype(BF16)
    q0, q1 = q[:, :d], q[:, d:]

    def body(c, carry):
        off = pl.multiple_of(c * tk, tk)
        k = k_ref[pl.ds(off, tk), :].astype(BF16)
        v = v_ref[pl.ds(off, tk), :].astype(BF16)
        first = _online_softmax_step(q0, k[:, :d], v, *carry[:3])
        second = _online_softmax_step(q1, k[:, d:], v, *carry[3:])
        return first + second

    m0, l0, acc0, m1, l1, acc1 = lax.fori_loop(0, t // tk, body, _softmax_start(bq, dv) + _softmax_start(bq, dv))
    lam = (jnp.exp(jnp.sum(lq1_ref[...] * lk1_ref[...], keepdims=True))
           - jnp.exp(jnp.sum(lq2_ref[...] * lk2_ref[...], keepdims=True)) + lam_init)
    o = acc0 / l0 - lam * (acc1 / l1)
    ms = jnp.mean(o * o, axis=-1, keepdims=True)
    o = o * lax.rsqrt(ms + EPS) * g_ref[...] * (1.0 - lam_init)
    o_ref[...] = o.astype(o_ref.dtype)


def _diff_attention(q2d, k2d, v2d, lams, subln_g, *, n_batch, s, t, n_heads, d, lam_init,
                    q_row0=0, kv_row0=0, k_col0=0, v_col0=0):
    w = 2 * d
    bq = _tile(Q_ROWS_GROUPED, s)
    tk = _tile(KV_CHUNK, t)
    assert q_row0 % bq == 0 and kv_row0 % t == 0 and k_col0 % w == 0 and v_col0 % w == 0
    qb0, kb0, kc0, vc0 = q_row0 // bq, kv_row0 // t, k_col0 // w, v_col0 // w
    nq = s // bq
    kernel = functools.partial(_diff_attn_kernel, d=d, scale=d ** -0.5, tk=tk, lam_init=lam_init)
    vec = pl.BlockSpec((1, d), lambda b, h, i: (0, 0))
    return pl.pallas_call(
        kernel,
        grid=(n_batch, n_heads, nq),
        in_specs=[pl.BlockSpec((bq, w), lambda b, h, i: (qb0 + b * nq + i, h)),
                  pl.BlockSpec((t, w), lambda b, h, i: (kb0 + b, kc0 + h)),
                  pl.BlockSpec((t, w), lambda b, h, i: (kb0 + b, vc0 + h)),
                  vec, vec, vec, vec,
                  pl.BlockSpec((1, w), lambda b, h, i: (0, 0))],
        out_specs=pl.BlockSpec((bq, w), lambda b, h, i: (b * nq + i, h)),
        out_shape=jax.ShapeDtypeStruct((n_batch * s, n_heads * w), BF16),
        compiler_params=_params("arbitrary", "arbitrary", "arbitrary"),
        name="diff_attention",
    )(q2d, k2d, v2d, *[a.reshape(1, d) for a in lams], subln_g.reshape(1, w))


def _with_cache(cache2d, new2d, n_req, dec_seq):
    w = new2d.shape[-1]
    both = jnp.concatenate([cache2d.reshape(n_req, -1, w), new2d.reshape(n_req, dec_seq, w)], axis=1)
    return both.astype(BF16).reshape(-1, w)


def _gqa_mixer(geom, dims, h, p, cache):
    wqkv, qg, kg, _ = p
    b, s, n_req, dec_seq = dims
    d = wqkv.shape[0]
    hd = qg.shape[0]
    n_heads = d // hd
    n_kv = (wqkv.shape[1] - d) // (2 * hd)
    kv_w = n_kv * hd
    past = cache[0].shape[1]
    bm = geom.row_tile(ROW_TILE)
    cos_tab, sin_tab = _rope_table(dec_seq, hd, hd, 0, bm)
    gains = jnp.concatenate([jnp.tile(qg, n_heads), jnp.tile(kg, n_kv), jnp.ones((kv_w,), F32)]).reshape(1, -1)
    qkv = _matmul_heads(geom, h, wqkv, gains, cos_tab, sin_tab, head_width=hd, n_valid=hd,
                        n_norm_cols=d + kv_w, rb=hd // 4, rope_slabs=(0,), out_dtype=F32, bm=bm)
    common = dict(n_groups=n_kv, n_rep=n_heads // n_kv, dk=hd, dv=hd, scale=hd ** -0.5)
    o_ctx = _attention(qkv, qkv, qkv, n_batch=b, s=s, t=s, k_col0=d, v_col0=d + kv_w, **common)
    k_all = _with_cache(cache[0], qkv[geom.n_ctx:, d:d + kv_w], n_req, dec_seq)
    v_all = _with_cache(cache[1], qkv[geom.n_ctx:, d + kv_w:], n_req, dec_seq)
    o_lat = _attention(qkv, k_all, v_all, n_batch=n_req, s=dec_seq, t=past + dec_seq,
                       q_row0=geom.n_ctx, **common)
    state = (qkv[:geom.n_ctx, d:d + kv_w].reshape(b, s, n_kv, hd),
             qkv[:geom.n_ctx, d + kv_w:].reshape(b, s, n_kv, hd))
    return jnp.concatenate([o_ctx, o_lat], axis=0), state


def _diff_mixer(geom, dims, h, p, cache, lam_init):
    wqkv, qg, kg, lq1, lk1, lq2, lk2, subln_g, _ = p
    b, s, n_req, dec_seq = dims
    d = wqkv.shape[0]
    hd = qg.shape[0]
    n_heads = d // (2 * hd)
    past = cache[0].shape[1]
    bm = geom.row_tile(ROW_TILE)
    cos_tab, sin_tab = _rope_table(dec_seq, hd, hd, 0, bm)
    gains = jnp.concatenate([jnp.tile(qg, 2 * n_heads), jnp.tile(kg, 2 * n_heads), jnp.ones((d,), F32)]).reshape(1, -1)
    qkv = _matmul_heads(geom, h, wqkv, gains, cos_tab, sin_tab, head_width=hd, n_valid=hd,
                        n_norm_cols=2 * d, rb=hd // 4, rope_slabs=(0,), out_dtype=F32, bm=bm)
    common = dict(n_heads=n_heads, d=hd, lam_init=lam_init)
    lams = (lq1, lk1, lq2, lk2)
    o_ctx = _diff_attention(qkv, qkv, qkv, lams, subln_g, n_batch=b, s=s, t=s, k_col0=d, v_col0=2 * d, **common)
    k_all = _with_cache(cache[0], qkv[geom.n_ctx:, d:2 * d], n_req, dec_seq)
    v_all = _with_cache(cache[1], qkv[geom.n_ctx:, 2 * d:], n_req, dec_seq)
    o_lat = _diff_attention(qkv, k_all, v_all, lams, subln_g, n_batch=n_req, s=dec_seq, t=past + dec_seq,
                            q_row0=geom.n_ctx, **common)
    state = (qkv[:geom.n_ctx, d:2 * d].reshape(b, s, n_heads, 2, hd),
             qkv[:geom.n_ctx, 2 * d:].reshape(b, s, n_heads, 2 * hd))
    return jnp.concatenate([o_ctx, o_lat], axis=0), state


def _mla_mixer(geom, dims, h, p, cache):
    wdown, qa_g, wuq, kva_g, wukv, qg, kg, wo = p
    b, s, n_req, dec_seq = dims
    q_rank, kv_rank = wuq.shape[0], wukv.shape[0]
    rope_dim = wdown.shape[1] - q_rank - kv_rank
    qk_dim = qg.shape[0]
    nope = qk_dim - rope_dim
    v_dim = LANES
    n_heads = wo.shape[0] // v_dim
    assert nope == LANES and rope_dim <= LANES and wukv.shape[1] == n_heads * 2 * LANES
    assert q_rank % kv_rank == 0
    hw = 2 * LANES
    past = cache[0].shape[1]
    t = past + dec_seq
    n_ctx = geom.n_ctx

    n_down = wdown.shape[1]
    pad_cols = -n_down % COL_TILE
    down = _matmul_plain(h, jnp.pad(wdown, ((0, 0), (0, pad_cols))), F32)
    cq = _plain_norm(geom, down, 0, q_rank, qa_g, BF16)
    ckv = _plain_norm(geom, down, q_rank // kv_rank, kv_rank, kva_g, F32)
    kpe = down[:, q_rank + kv_rank:n_down]

    wuq_p = jnp.pad(wuq.reshape(q_rank, n_heads, qk_dim), ((0, 0), (0, 0), (0, hw - qk_dim))).reshape(q_rank, n_heads * hw)
    gq = jnp.tile(jnp.pad(qg, (0, hw - qk_dim)), n_heads).reshape(1, -1)
    bm = geom.row_tile(ROW_TILE)
    cos_q, sin_q = _rope_table(dec_seq, rope_dim, hw, nope, bm)
    q = _matmul_heads(geom, cq, wuq_p, gq, cos_q, sin_q, head_width=hw, n_valid=qk_dim,
                      n_norm_cols=n_heads * hw, rb=rope_dim // 4, rope_slabs=(1,), out_dtype=BF16, bm=bm)

    def rows(cached, new, width):
        lat = jnp.concatenate([cached.reshape(n_req, past, width), new[n_ctx:].reshape(n_req, dec_seq, width)], axis=1)
        return jnp.concatenate([lat.reshape(n_req * t, width), new[:n_ctx]], axis=0)

    ckv_all = rows(cache[0], ckv, kv_rank).astype(BF16)
    kpe_all = jnp.pad(rows(cache[1], kpe, rope_dim), ((0, 0), (0, LANES - rope_dim)))
    bmk = _tile(COL_TILE, past, dec_seq, n_ctx)
    per_req, past_blocks = t // bmk, past // bmk

    def rope_block(i):
        r = i % per_req
        return jnp.where((i < n_req * per_req) & (r >= past_blocks), 1 + r - past_blocks, 0)

    cos_k, sin_k = _rope_table(dec_seq, rope_dim, LANES, 0, bmk)
    kg_nope = kg[:nope].reshape(1, LANES)
    kg_rope = jnp.pad(kg[nope:], (0, LANES - rope_dim)).reshape(1, LANES)
    k_all, v_all = _matmul_mla_kv(ckv_all, wukv, kpe_all, cos_k, sin_k, kg_nope, kg_rope, rope_block,
                                  bm=bmk, rb=rope_dim // 4, n_valid=qk_dim)

    common = dict(n_groups=n_heads, n_rep=1, dk=hw, dv=v_dim, scale=qk_dim ** -0.5)
    o_ctx = _attention(q, k_all, v_all, n_batch=b, s=s, t=s, kv_row0=n_req * t, **common)
    o_lat = _attention(q, k_all, v_all, n_batch=n_req, s=dec_seq, t=t, q_row0=n_ctx, **common)
    state = (ckv[:n_ctx].reshape(b, s, kv_rank), kpe[:n_ctx].reshape(b, s, rope_dim))
    return jnp.concatenate([o_ctx, o_lat], axis=0), state


_MIXER_ARGS = (4, 9, 8)


def kernel(x_prompt, x_sample, c, cache_k0, cache_v0, cache_k1, cache_v1, cache_ckv2, cache_kpe2, cache_k3, cache_v3, c_ctx, ada_w0, ada_b0, norm1_g0, norm2_g0, ffn_w13_0, ffn_w2_0, gqa_wqkv0, gqa_qg0, gqa_kg0, gqa_wo0, ada_w1, ada_b1, norm1_g1, norm2_g1, ffn_w13_1, ffn_w2_1, diff_wqkv1, diff_qg1, diff_kg1, diff_lq1_1, diff_lk1_1, diff_lq2_1, diff_lk2_1, diff_subln_g1, diff_wo1, ada_w2, ada_b2, norm1_g2, norm2_g2, ffn_w13_2, ffn_w2_2, mla_wdown2, mla_qa_g2, mla_wuq2, mla_kva_g2, mla_wukv2, mla_qg2, mla_kg2, mla_wo2, ada_w3, ada_b3, norm1_g3, norm2_g3, ffn_w13_3, ffn_w2_3, gqa_wqkv3, gqa_qg3, gqa_kg3, gqa_wo3):
    layers = [
        ((ada_w0, ada_b0, norm1_g0, norm2_g0, ffn_w13_0, ffn_w2_0),
         (gqa_wqkv0, gqa_qg0, gqa_kg0, gqa_wo0), (cache_k0, cache_v0)),
        ((ada_w1, ada_b1, norm1_g1, norm2_g1, ffn_w13_1, ffn_w2_1),
         (diff_wqkv1, diff_qg1, diff_kg1, diff_lq1_1, diff_lk1_1, diff_lq2_1, diff_lk2_1, diff_subln_g1, diff_wo1),
         (cache_k1, cache_v1)),
        ((ada_w2, ada_b2, norm1_g2, norm2_g2, ffn_w13_2, ffn_w2_2),
         (mla_wdown2, mla_qa_g2, mla_wuq2, mla_kva_g2, mla_wukv2, mla_qg2, mla_kg2, mla_wo2),
         (cache_ckv2, cache_kpe2)),
        ((ada_w3, ada_b3, norm1_g3, norm2_g3, ffn_w13_3, ffn_w2_3),
         (gqa_wqkv3, gqa_qg3, gqa_kg3, gqa_wo3), (cache_k3, cache_v3)),
    ]
    b, s, d = x_prompt.shape
    n_req, dec_seq, _ = x_sample.shape
    assert dec_seq % GRID_W == 0
    geom = _Geom(b * s, n_req, dec_seq)
    dims = (b, s, n_req, dec_seq)
    n_groups = 1 + n_req
    assert n_groups <= SUBLANES

    x = jnp.concatenate([x_prompt.reshape(b * s, d), x_sample.reshape(n_req * dec_seq, d)], axis=0)
    cs = jnp.concatenate([c_ctx[None, :], c, jnp.zeros((SUBLANES - n_groups, d), F32)], axis=0)

    new_state = []
    for l, (common, mixer, cache) in enumerate(layers):
        ada_w, ada_b, g1, g2, w13, w2 = common
        mod = _modulation(cs, ada_w, ada_b)
        mod3 = mod[:n_groups].reshape(n_groups * N_MOD, 1, d)
        h = _norm_modulate(geom, x, 0, d, g1, mod3, 0, 1, BF16)
        kind = l % N_MIXERS
        if kind == 0:
            o, st = _gqa_mixer(geom, dims, h, mixer, cache)
        elif kind == 1:
            o, st = _diff_mixer(geom, dims, h, mixer, cache, 0.8 - 0.6 * math.exp(-0.3 * l))
        else:
            o, st = _mla_mixer(geom, dims, h, mixer, cache)
        new_state.extend(st)
        x = _matmul_gated_residual(geom, o, mixer[-1], x, mod3, 2)
        h = _norm_modulate(geom, x, 0, d, g2, mod3, 3, 4, BF16)
        hidden = _matmul_swiglu(h, w13)
        x = _matmul_gated_residual(geom, hidden, w2.astype(BF16), x, mod3, 5, row_target=ROW_TILE // 2)

    y_prompt = x[:b * s].reshape(b, s, d)
    y_sample = x[b * s:].reshape(n_req, dec_seq, d)
    return (y_prompt, y_sample, *new_state)
```

```python
import functools
import math

import jax
import jax.numpy as jnp
from jax import lax
from jax.experimental import pallas as pl
from jax.experimental.pallas import tpu as pltpu

F32 = jnp.float32
BF16 = jnp.bfloat16

EPS = 1e-6
ROPE_THETA = 10000.0
GRID_W = 64
N_MOD = 6
N_MIXERS = 3
LANES = 128
SUBLANES = 8
VMEM_LIMIT_BYTES = 56 * 1024 * 1024
NEG_BIG = -1e30

ROW_TILE = 1024
COL_TILE = 512
SWIGLU_TILE = 256
NORM_ROWS = 256
MOD_COLS = 1024
KV_CHUNK = 512
Q_ROWS_GROUPED = 256
Q_ROWS_SINGLE = 512
SOFTMAX_BLOCK_ELEMS = 16 * SUBLANES * LANES


def _params(*sem):
    return pltpu.CompilerParams(dimension_semantics=sem, vmem_limit_bytes=VMEM_LIMIT_BYTES)


def _tile(target, n, *also):
    t = min(target, n)
    while any(d % t for d in (n,) + also):
        t //= 2
    assert t >= 1
    return t


def _kv_chunk(t):
    for cand in range(min(KV_CHUNK, t) // LANES * LANES, 0, -LANES):
        if t % cand == 0:
            return cand
    return t


class _Geom:
    def __init__(self, n_ctx, n_req, dec_seq):
        self.n_ctx, self.n_req, self.dec_seq = n_ctx, n_req, dec_seq
        self.m = n_ctx + n_req * dec_seq

    def row_tile(self, target):
        return _tile(target, self.n_ctx, self.dec_seq)

    def group_of_block(self, i, bm):
        nc, per = self.n_ctx // bm, self.dec_seq // bm
        return jnp.where(i < nc, 0, 1 + (i - nc) // per)

    def rope_block(self, i, bm):
        nc, per = self.n_ctx // bm, self.dec_seq // bm
        return jnp.where(i < nc, 0, 1 + (i - nc) % per)


def _rope_cos_sin(n_tokens, rot_dim):
    rows = n_tokens // GRID_W
    row_pos = jnp.repeat(jnp.arange(rows, dtype=F32), GRID_W)
    col_pos = jnp.tile(jnp.arange(GRID_W, dtype=F32), rows)
    half = rot_dim // 2
    inv = ROPE_THETA ** (-jnp.arange(0, half, 2, dtype=F32) / half)
    ang_r = row_pos[:, None] * inv
    ang_c = col_pos[:, None] * inv
    cos = jnp.concatenate([jnp.cos(ang_r), jnp.cos(ang_r), jnp.cos(ang_c), jnp.cos(ang_c)], axis=-1)
    sin = jnp.concatenate([-jnp.sin(ang_r), jnp.sin(ang_r), -jnp.sin(ang_c), jnp.sin(ang_c)], axis=-1)
    return cos, sin


def _rope_table(n_tokens, rot_dim, width, lane0, bm):
    cos, sin = _rope_cos_sin(n_tokens, rot_dim)
    cos_full = jnp.ones((n_tokens, width), F32).at[:, lane0:lane0 + rot_dim].set(cos)
    sin_full = jnp.zeros((n_tokens, width), F32).at[:, lane0:lane0 + rot_dim].set(sin)
    cos_tab = jnp.concatenate([jnp.ones((bm, width), F32), cos_full], axis=0)
    sin_tab = jnp.concatenate([jnp.zeros((bm, width), F32), sin_full], axis=0)
    return cos_tab, sin_tab


def _rope_slab(y, cos, sin, rb):
    lane = lax.broadcasted_iota(jnp.int32, y.shape, 1)
    first = ((lane // rb) % 2) == 0
    partner = jnp.where(first, pltpu.roll(y, LANES - rb, 1), pltpu.roll(y, rb, 1))
    return y * cos + partner * sin


def _mod_kernel(c_ref, w_ref, b_ref, o_ref):
    a = jax.nn.silu(c_ref[...]).astype(BF16)
    o_ref[...] = jnp.dot(a, w_ref[...].astype(BF16), preferred_element_type=F32) + b_ref[...]


def _modulation(cs, w, b):
    d, n = w.shape
    bn = _tile(MOD_COLS, n)
    return pl.pallas_call(
        _mod_kernel,
        grid=(n // bn,),
        in_specs=[pl.BlockSpec((SUBLANES, d), lambda j: (0, 0)),
                  pl.BlockSpec((d, bn), lambda j: (0, j)),
                  pl.BlockSpec((1, bn), lambda j: (0, j))],
        out_specs=pl.BlockSpec((SUBLANES, bn), lambda j: (0, j)),
        out_shape=jax.ShapeDtypeStruct((SUBLANES, n), F32),
        compiler_params=_params("arbitrary"),
        name="adaln_modulation",
    )(cs, w, b.reshape(1, n))


def _norm_kernel(x_ref, g_ref, sh_ref, sc_ref, o_ref):
    x = x_ref[...]
    ms = jnp.mean(x * x, axis=-1, keepdims=True)
    y = x * lax.rsqrt(ms + EPS) * g_ref[...]
    o_ref[...] = (y * (1.0 + sc_ref[...]) + sh_ref[...]).astype(o_ref.dtype)


def _norm_modulate(geom, x, col_block, width, g, mod3, shift_kind, scale_kind, out_dtype):
    m = x.shape[0]
    bm = geom.row_tile(NORM_ROWS)

    def mod_spec(kind):
        return pl.BlockSpec((None, 1, width), lambda i: (geom.group_of_block(i, bm) * N_MOD + kind, 0, 0))

    return pl.pallas_call(
        _norm_kernel,
        grid=(m // bm,),
        in_specs=[pl.BlockSpec((bm, width), lambda i: (i, col_block)),
                  pl.BlockSpec((1, width), lambda i: (0, 0)),
                  mod_spec(shift_kind), mod_spec(scale_kind)],
        out_specs=pl.BlockSpec((bm, width), lambda i: (i, 0)),
        out_shape=jax.ShapeDtypeStruct((m, width), out_dtype),
        compiler_params=_params("arbitrary"),
        name="rmsnorm_modulate",
    )(x, g.reshape(1, width), mod3, mod3)


def _plain_norm(geom, x, col_block, width, g, out_dtype):
    zeros = jnp.zeros((N_MOD * (1 + geom.n_req), 1, width), F32)
    return _norm_modulate(geom, x, col_block, width, g, zeros, 0, 1, out_dtype)


def _matmul_call(a, weights, extras, extra_specs, out_shapes, out_specs, epilogue, *, bm, bn, n_cols, name):
    m, k = a.shape
    n_w, n_ex, n_out = len(weights), len(extras), len(out_shapes)
    staged = n_w > 1 or weights[0][0].dtype != BF16

    def kernel(*refs):
        a_ref = refs[0]
        w_refs = refs[1:1 + n_w]
        ex_refs = refs[1 + n_w:1 + n_w + n_ex]
        out_refs = refs[1 + n_w + n_ex:1 + n_w + n_ex + n_out]
        if staged:
            w_bf16 = refs[-1]

            @pl.when(pl.program_id(1) == 0)
            def _():
                for idx, w_ref in enumerate(w_refs):
                    w_bf16[:, idx * bn:(idx + 1) * bn] = w_ref[...].astype(BF16)

            w_val = w_bf16[...]
        else:
            w_val = w_refs[0][...]
        acc = jnp.dot(a_ref[...], w_val, preferred_element_type=F32)
        epilogue([acc[:, idx * bn:(idx + 1) * bn] for idx in range(n_w)], ex_refs, out_refs)

    in_specs = [pl.BlockSpec((bm, k), lambda j, i: (i, 0))]
    for _, off in weights:
        in_specs.append(pl.BlockSpec((k, bn), lambda j, i, off=off: (0, off + j)))
    in_specs += list(extra_specs)
    return pl.pallas_call(
        kernel,
        grid=(n_cols // bn, m // bm),
        in_specs=in_specs,
        out_specs=out_specs,
        out_shape=out_shapes,
        scratch_shapes=[pltpu.VMEM((k, n_w * bn), BF16)] if staged else [],
        compiler_params=_params("arbitrary", "arbitrary"),
        name=name,
    )(a, *[w for w, _ in weights], *extras)


def _matmul_plain(a, w, out_dtype):
    m, n = a.shape[0], w.shape[1]
    bm, bn = _tile(ROW_TILE, m), _tile(COL_TILE, n)

    def epilogue(accs, ex, outs):
        outs[0][...] = accs[0].astype(out_dtype)

    return _matmul_call(a, [(w, 0)], [], [], [jax.ShapeDtypeStruct((m, n), out_dtype)],
                        [pl.BlockSpec((bm, bn), lambda j, i: (i, j))], epilogue,
                        bm=bm, bn=bn, n_cols=n, name="matmul")[0]


def _matmul_gated_residual(geom, a, w, res, mod3, gate_kind, row_target=ROW_TILE):
    m, n = a.shape[0], w.shape[1]
    bm, bn = geom.row_tile(row_target), _tile(COL_TILE, n)

    def epilogue(accs, ex, outs):
        res_ref, gate_ref = ex
        outs[0][...] = res_ref[...] + gate_ref[...] * accs[0]

    extra_specs = [pl.BlockSpec((bm, bn), lambda j, i: (i, j)),
                   pl.BlockSpec((None, 1, bn), lambda j, i: (geom.group_of_block(i, bm) * N_MOD + gate_kind, 0, j))]
    return _matmul_call(a, [(w, 0)], [res, mod3], extra_specs, [jax.ShapeDtypeStruct((m, n), F32)],
                        [pl.BlockSpec((bm, bn), lambda j, i: (i, j))], epilogue,
                        bm=bm, bn=bn, n_cols=n, name="matmul_gated_residual")[0]


def _matmul_swiglu(a, w13):
    m, f = a.shape[0], w13.shape[1] // 2
    bm, bn = _tile(ROW_TILE, m), _tile(SWIGLU_TILE, f)

    def epilogue(accs, ex, outs):
        outs[0][...] = (jax.nn.silu(accs[0]) * accs[1]).astype(BF16)

    return _matmul_call(a, [(w13, 0), (w13, f // bn)], [], [], [jax.ShapeDtypeStruct((m, f), BF16)],
                        [pl.BlockSpec((bm, bn), lambda j, i: (i, j))], epilogue,
                        bm=bm, bn=bn, n_cols=f, name="matmul_swiglu")[0]


def _matmul_heads(geom, a, w, col0, n_cols, gain_cols, cos_tab, sin_tab, *, head_width, n_valid, rb,
                  rope_slabs, bm, normed=True, with_state=False):
    m = a.shape[0]
    bn = _tile(COL_TILE, n_cols, col0)
    assert bn % head_width == 0 and head_width % LANES == 0
    n_ctx_blocks = geom.n_ctx // bm

    def epilogue(accs, ex, outs):
        acc = accs[0]
        gain_ref, cos_ref, sin_ref = ex
        is_ctx = pl.program_id(1) < n_ctx_blocks
        for h in range(bn // head_width):
            lo = h * head_width
            y = acc[:, lo:lo + head_width]
            if normed:
                ms = jnp.sum(y * y, axis=-1, keepdims=True) / n_valid
                y = y * lax.rsqrt(ms + EPS) * gain_ref[:, lo:lo + head_width]
            for s in range(head_width // LANES):
                ys = y[:, s * LANES:(s + 1) * LANES]
                if normed and s in rope_slabs:
                    ys = _rope_slab(ys, cos_ref[:, s * LANES:(s + 1) * LANES],
                                    sin_ref[:, s * LANES:(s + 1) * LANES], rb)
                cols = slice(lo + s * LANES, lo + (s + 1) * LANES)
                outs[0][:, cols] = ys.astype(BF16)
                if with_state:
                    @pl.when(is_ctx)
                    def _(ys=ys, cols=cols):
                        outs[1][:, cols] = ys

    extra_specs = [pl.BlockSpec((1, bn), lambda j, i: (0, j)),
                   pl.BlockSpec((bm, head_width), lambda j, i: (geom.rope_block(i, bm), 0)),
                   pl.BlockSpec((bm, head_width), lambda j, i: (geom.rope_block(i, bm), 0))]
    out_shapes = [jax.ShapeDtypeStruct((m, n_cols), BF16)]
    out_specs = [pl.BlockSpec((bm, bn), lambda j, i: (i, j))]
    if with_state:
        out_shapes.append(jax.ShapeDtypeStruct((geom.n_ctx, n_cols), F32))
        out_specs.append(pl.BlockSpec((bm, bn), lambda j, i: (jnp.minimum(i, n_ctx_blocks - 1), j)))
    return _matmul_call(a, [(w, col0 // bn)], [gain_cols, cos_tab, sin_tab], extra_specs,
                        out_shapes, out_specs, epilogue,
                        bm=bm, bn=bn, n_cols=n_cols, name="matmul_head_norm_rope")


def _matmul_mla_kv(ckv, w, kpe, cos_tab, sin_tab, kg_nope, kg_rope, rope_block, *, bm, rb, n_valid):
    m, n = ckv.shape[0], w.shape[1]
    hw = 2 * LANES
    bn = _tile(COL_TILE, n)
    assert bn % hw == 0

    def epilogue(accs, ex, outs):
        acc = accs[0]
        kpe_ref, cos_ref, sin_ref, gn_ref, gr_ref = ex
        k_ref, v_ref = outs
        pe = kpe_ref[...]
        ss_pe = jnp.sum(pe * pe, axis=-1, keepdims=True)
        pe_rot = _rope_slab(pe * gr_ref[...], cos_ref[...], sin_ref[...], rb)
        for h in range(bn // hw):
            kn = acc[:, h * hw:h * hw + LANES]
            ms = (jnp.sum(kn * kn, axis=-1, keepdims=True) + ss_pe) / n_valid
            r = lax.rsqrt(ms + EPS)
            k_ref[:, h * hw:h * hw + LANES] = (kn * r * gn_ref[...]).astype(BF16)
            k_ref[:, h * hw + LANES:(h + 1) * hw] = (pe_rot * r).astype(BF16)
            v_ref[:, h * LANES:(h + 1) * LANES] = acc[:, h * hw + LANES:(h + 1) * hw].astype(BF16)

    extra_specs = [pl.BlockSpec((bm, LANES), lambda j, i: (i, 0)),
                   pl.BlockSpec((bm, LANES), lambda j, i: (rope_block(i), 0)),
                   pl.BlockSpec((bm, LANES), lambda j, i: (rope_block(i), 0)),
                   pl.BlockSpec((1, LANES), lambda j, i: (0, 0)),
                   pl.BlockSpec((1, LANES), lambda j, i: (0, 0))]
    return _matmul_call(ckv, [(w, 0)], [kpe, cos_tab, sin_tab, kg_nope, kg_rope], extra_specs,
                        [jax.ShapeDtypeStruct((m, n), BF16), jax.ShapeDtypeStruct((m, n // 2), BF16)],
                        [pl.BlockSpec((bm, bn), lambda j, i: (i, j)),
                         pl.BlockSpec((bm, bn // 2), lambda j, i: (i, j))], epilogue,
                        bm=bm, bn=bn, n_cols=n, name="matmul_mla_kv")


class _Softmax:
    N_SCRATCH = 6

    def __init__(self, s, p, m, l, alpha, acc):
        self.s, self.p, self.m, self.l, self.alpha, self.acc = s, p, m, l, alpha, acc

    @staticmethod
    def scratch(rows, tk, dv):
        assert tk % LANES == 0 and dv % LANES == 0
        return [pltpu.VMEM((rows, tk), F32), pltpu.VMEM((rows, tk), BF16),
                pltpu.VMEM((rows, LANES), F32), pltpu.VMEM((rows, LANES), F32), pltpu.VMEM((rows, LANES), F32),
                pltpu.VMEM((rows, dv), F32)]

    def start(self):
        self.m[...] = jnp.full(self.m.shape, NEG_BIG, F32)
        self.l[...] = jnp.zeros(self.l.shape, F32)
        self.acc[...] = jnp.zeros(self.acc.shape, F32)

    def scores(self, q, k):
        self.s[...] = lax.dot_general(q, k, (((1,), (1,)), ((), ())), preferred_element_type=F32)

    def update(self):
        rows, tk = self.s.shape
        rb = _tile(max(SUBLANES, 1 << ((SOFTMAX_BLOCK_ELEMS // tk).bit_length() - 1)), rows)
        blocks = [pl.ds(r * rb, rb) for r in range(rows // rb)]
        slabs = [pl.ds(j * LANES, LANES) for j in range(tk // LANES)]
        for sl in blocks:
            mx = self.s[sl, slabs[0]]
            for cols in slabs[1:]:
                mx = jnp.maximum(mx, self.s[sl, cols])
            m_old = self.m[sl, :]
            m_new = jnp.maximum(m_old, jnp.broadcast_to(jnp.max(mx, axis=1, keepdims=True), m_old.shape))
            self.alpha[sl, :] = jnp.exp(m_old - m_new)
            self.m[sl, :] = m_new
        for sl in blocks:
            m = self.m[sl, :]
            total = None
            for cols in slabs:
                p = jnp.exp(self.s[sl, cols] - m)
                total = p if total is None else total + p
                self.p[sl, cols] = p.astype(BF16)
            self.l[sl, :] = self.alpha[sl, :] * self.l[sl, :] + total

    def accumulate(self, v):
        pv = jnp.dot(self.p[...], v, preferred_element_type=F32)
        alpha = self.alpha[...]
        for j in range(self.acc.shape[1] // LANES):
            cols = pl.ds(j * LANES, LANES)
            self.acc[:, cols] = alpha * self.acc[:, cols] + pv[:, j * LANES:(j + 1) * LANES]

    def result(self, lo, n):
        inv = 1.0 / jnp.sum(self.l[lo:lo + n, :], axis=1, keepdims=True)
        return self.acc[lo:lo + n, :] * inv


def _attn_kernel(q_ref, k_ref, v_ref, o_ref, *scratch, n_rep, dk, dv, tk):
    bq, t = q_ref.shape[0], k_ref.shape[0]
    sm = _Softmax(*scratch[:_Softmax.N_SCRATCH])
    if n_rep > 1:
        q_all = scratch[_Softmax.N_SCRATCH]
        for r in range(n_rep):
            q_all[r * bq:(r + 1) * bq, :] = q_ref[:, r * dk:(r + 1) * dk]
    else:
        q_all = q_ref
    sm.start()

    def body(c, carry):
        off = pl.multiple_of(c * tk, tk)
        sm.scores(q_all[...], k_ref[pl.ds(off, tk), :])
        sm.update()
        sm.accumulate(v_ref[pl.ds(off, tk), :])
        return carry

    lax.fori_loop(0, t // tk, body, 0)
    for r in range(n_rep):
        o_ref[:, r * dv:(r + 1) * dv] = sm.result(r * bq, bq).astype(o_ref.dtype)


def _attention(q2d, k2d, v2d, *, n_batch, s, t, n_groups, n_rep, dk, dv, q_row0=0, kv_row0=0):
    bq = _tile(Q_ROWS_GROUPED if n_rep > 1 else Q_ROWS_SINGLE, s)
    tk = _kv_chunk(t)
    qw, ow = n_rep * dk, n_rep * dv
    assert q_row0 % bq == 0 and kv_row0 % t == 0
    qb0, kb0 = q_row0 // bq, kv_row0 // t
    nq = s // bq
    rows = n_rep * bq
    scratch = _Softmax.scratch(rows, tk, dv)
    if n_rep > 1:
        scratch.append(pltpu.VMEM((rows, dk), BF16))
    kernel = functools.partial(_attn_kernel, n_rep=n_rep, dk=dk, dv=dv, tk=tk)
    return pl.pallas_call(
        kernel,
        grid=(n_batch, n_groups, nq),
        in_specs=[pl.BlockSpec((bq, qw), lambda b, g, i: (qb0 + b * nq + i, g)),
                  pl.BlockSpec((t, dk), lambda b, g, i: (kb0 + b, g)),
                  pl.BlockSpec((t, dv), lambda b, g, i: (kb0 + b, g))],
        out_specs=pl.BlockSpec((bq, ow), lambda b, g, i: (b * nq + i, g)),
        out_shape=jax.ShapeDtypeStruct((n_batch * s, n_groups * ow), BF16),
        scratch_shapes=scratch,
        compiler_params=_params("arbitrary", "arbitrary", "arbitrary"),
        name="attention",
    )(q2d, k2d, v2d)


def _diff_attn_kernel(q_ref, k_ref, v_ref, lq1_ref, lk1_ref, lq2_ref, lk2_ref, g_ref, o_ref, *scratch,
                      d, tk, lam_init):
    bq, t = q_ref.shape[0], k_ref.shape[0]
    maps = (_Softmax(*scratch[:_Softmax.N_SCRATCH]), _Softmax(*scratch[_Softmax.N_SCRATCH:]))
    for sm in maps:
        sm.start()

    def scores(off):
        for i, sm in enumerate(maps):
            sm.scores(q_ref[:, i * d:(i + 1) * d], k_ref[pl.ds(off, tk), i * d:(i + 1) * d])

    def accumulate(off):
        v = v_ref[pl.ds(off, tk), :]
        for sm in maps:
            sm.accumulate(v)

    def body(c, carry):
        off = pl.multiple_of(c * tk, tk)
        scores(off)
        for sm in maps:
            sm.update()
        accumulate(off)
        return carry

    lax.fori_loop(0, t // tk, body, 0)
    lam =(jnp.exp(jnp.sum(lq1_ref[...] * lk1_ref[...], keepdims=True))
           - jnp.exp(jnp.sum(lq2_ref[...] * lk2_ref[...], keepdims=True)) + lam_init)
    o = maps[0].result(0, bq) - lam * maps[1].result(0, bq)
    ms = jnp.mean(o * o, axis=-1, keepdims=True)
    o = o * lax.rsqrt(ms + EPS) * g_ref[...] * (1.0 - lam_init)
    o_ref[...] = o.astype(o_ref.dtype)


def _diff_attention(q2d, k2d, v2d, lams, subln_g, *, n_batch, s, t, n_heads, d, lam_init, q_row0=0, kv_row0=0):
    w = 2 * d
    bq = _tile(Q_ROWS_SINGLE, s)
    tk = _kv_chunk(t)
    assert q_row0 % bq == 0 and kv_row0 % t == 0
    qb0, kb0 = q_row0 // bq, kv_row0 // t
    nq = s // bq
    kernel = functools.partial(_diff_attn_kernel, d=d, tk=tk, lam_init=lam_init)
    vec = pl.BlockSpec((1, d), lambda b, h, i: (0, 0))
    return pl.pallas_call(
        kernel,
        grid=(n_batch, n_heads, nq),
        in_specs=[pl.BlockSpec((bq, w), lambda b, h, i: (qb0 + b * nq + i, h)),
                  pl.BlockSpec((t, w), lambda b, h, i: (kb0 + b, h)),
                  pl.BlockSpec((t, w), lambda b, h, i: (kb0 + b, h)),
                  vec, vec, vec, vec,
                  pl.BlockSpec((1, w), lambda b, h, i: (0, 0))],
        out_specs=pl.BlockSpec((bq, w), lambda b, h, i: (b * nq + i, h)),
        out_shape=jax.ShapeDtypeStruct((n_batch * s, n_heads * w), BF16),
        scratch_shapes=_Softmax.scratch(bq, tk, w) + _Softmax.scratch(bq, tk, w),
        compiler_params=_params("arbitrary", "arbitrary", "arbitrary"),
        name="diff_attention",
    )(q2d, k2d, v2d, *[a.reshape(1, d) for a in lams], subln_g.reshape(1, w))


def _with_cache(cache2d, new2d, n_req, dec_seq):
    w = new2d.shape[-1]
    both = jnp.concatenate([cache2d.reshape(n_req, -1, w).astype(BF16), new2d.reshape(n_req, dec_seq, w)], axis=1)
    return both.reshape(-1, w)


def _qkv_attention_inputs(geom, dims, h, wqkv, qg, kg, q_cols, k_cols, v_cols, hd, cache):
    b, s, n_req, dec_seq = dims
    bm = geom.row_tile(ROW_TILE)
    cos_tab, sin_tab = _rope_table(dec_seq, hd, hd, 0, bm)
    heads = dict(head_width=hd, n_valid=hd, rb=hd // 4, rope_slabs=(0,), bm=bm)
    q_gain = (jnp.tile(qg, q_cols // hd) * hd ** -0.5).reshape(1, -1)
    k_gain = jnp.tile(kg, k_cols // hd).reshape(1, -1)
    v_gain = jnp.ones((1, v_cols), F32)
    q, = _matmul_heads(geom, h, wqkv, 0, q_cols, q_gain, cos_tab, sin_tab, **heads)
    k, k_state = _matmul_heads(geom, h, wqkv, q_cols, k_cols, k_gain, cos_tab, sin_tab, with_state=True, **heads)
    v, v_state = _matmul_heads(geom, h, wqkv, q_cols + k_cols, v_cols, v_gain, cos_tab, sin_tab,
                               normed=False, with_state=True, **heads)
    k_all = _with_cache(cache[0], k[geom.n_ctx:], n_req, dec_seq)
    v_all = _with_cache(cache[1], v[geom.n_ctx:], n_req, dec_seq)
    return q, k, v, k_all, v_all, k_state, v_state


def _gqa_mixer(geom, dims, h, p, cache):
    wqkv, qg, kg, _ = p
    b, s, n_req, dec_seq = dims
    d = wqkv.shape[0]
    hd = qg.shape[0]
    n_heads = d // hd
    n_kv = (wqkv.shape[1] - d) // (2 * hd)
    kv_w = n_kv * hd
    past = cache[0].shape[1]
    q, k, v, k_all, v_all, k_state, v_state = _qkv_attention_inputs(
        geom, dims, h, wqkv, qg, kg, d, kv_w, kv_w, hd, cache)
    common = dict(n_groups=n_kv, n_rep=n_heads // n_kv, dk=hd, dv=hd)
    o_ctx = _attention(q, k, v, n_batch=b, s=s, t=s, **common)
    o_lat = _attention(q, k_all, v_all, n_batch=n_req, s=dec_seq, t=past + dec_seq, q_row0=geom.n_ctx, **common)
    state = (k_state.reshape(b, s, n_kv, hd), v_state.reshape(b, s, n_kv, hd))
    return jnp.concatenate([o_ctx, o_lat], axis=0), state


def _diff_mixer(geom, dims, h, p, cache, lam_init):
    wqkv, qg, kg, lq1, lk1, lq2, lk2, subln_g, _ = p
    b, s, n_req, dec_seq = dims
    d = wqkv.shape[0]
    hd = qg.shape[0]
    n_heads = d // (2 * hd)
    past = cache[0].shape[1]
    q, k, v, k_all, v_all, k_state, v_state = _qkv_attention_inputs(
        geom, dims, h, wqkv, qg, kg, d, d, d, hd, cache)
    common = dict(n_heads=n_heads, d=hd, lam_init=lam_init)
    lams = (lq1, lk1, lq2, lk2)
    o_ctx = _diff_attention(q, k, v, lams, subln_g, n_batch=b, s=s, t=s, **common)
    o_lat = _diff_attention(q, k_all, v_all, lams, subln_g, n_batch=n_req, s=dec_seq, t=past + dec_seq,
                            q_row0=geom.n_ctx, **common)
    state = (k_state.reshape(b, s, n_heads, 2, hd), v_state.reshape(b, s, n_heads, 2 * hd))
    return jnp.concatenate([o_ctx, o_lat], axis=0), state


def _mla_mixer(geom, dims, h, p, cache):
    wdown, qa_g, wuq, kva_g, wukv, qg, kg, wo = p
    b, s, n_req, dec_seq = dims
    q_rank, kv_rank = wuq.shape[0], wukv.shape[0]
    rope_dim = wdown.shape[1] - q_rank - kv_rank
    qk_dim = qg.shape[0]
    nope = qk_dim - rope_dim
    v_dim = LANES
    n_heads = wo.shape[0] // v_dim
    assert nope == LANES and rope_dim <= LANES and wukv.shape[1] == n_heads * 2 * LANES
    assert q_rank % kv_rank == 0
    hw = 2 * LANES
    past = cache[0].shape[1]
    t = past + dec_seq
    n_ctx = geom.n_ctx

    n_down = wdown.shape[1]
    pad_cols = -n_down % COL_TILE
    down = _matmul_plain(h, jnp.pad(wdown, ((0, 0), (0, pad_cols))), F32)
    cq = _plain_norm(geom, down, 0, q_rank, qa_g, BF16)
    ckv = _plain_norm(geom, down, q_rank // kv_rank, kv_rank, kva_g, F32)
    kpe = down[:, q_rank + kv_rank:n_down]

    wuq_p = jnp.pad(wuq.reshape(q_rank, n_heads, qk_dim), ((0, 0), (0, 0), (0, hw - qk_dim))).reshape(q_rank, n_heads * hw)
    gq = (jnp.tile(jnp.pad(qg, (0, hw - qk_dim)), n_heads) * qk_dim ** -0.5).reshape(1, -1)
    bm = geom.row_tile(ROW_TILE)
    cos_q, sin_q = _rope_table(dec_seq, rope_dim, hw, nope, bm)
    q, = _matmul_heads(geom, cq, wuq_p, 0, n_heads * hw, gq, cos_q, sin_q, head_width=hw, n_valid=qk_dim,
                       rb=rope_dim // 4, rope_slabs=(1,), bm=bm)

    def rows(cached, new, width):
        lat = jnp.concatenate([cached.reshape(n_req, past, width), new[n_ctx:].reshape(n_req, dec_seq, width)], axis=1)
        return jnp.concatenate([lat.reshape(n_req * t, width), new[:n_ctx]], axis=0)

    ckv_all = rows(cache[0], ckv, kv_rank).astype(BF16)
    kpe_all = jnp.pad(rows(cache[1], kpe, rope_dim), ((0, 0), (0, LANES - rope_dim)))
    bmk = _tile(COL_TILE, past, dec_seq, n_ctx)
    per_req, past_blocks = t // bmk, past // bmk

    def rope_block(i):
        r = i % per_req
        return jnp.where((i < n_req * per_req) & (r >= past_blocks), 1 + r - past_blocks, 0)

    cos_k, sin_k = _rope_table(dec_seq, rope_dim, LANES, 0, bmk)
    kg_nope = kg[:nope].reshape(1, LANES)
    kg_rope = jnp.pad(kg[nope:], (0, LANES - rope_dim)).reshape(1, LANES)
    k_all, v_all = _matmul_mla_kv(ckv_all, wukv, kpe_all, cos_k, sin_k, kg_nope, kg_rope, rope_block,
                                  bm=bmk, rb=rope_dim // 4, n_valid=qk_dim)

    common = dict(n_groups=n_heads, n_rep=1, dk=hw, dv=v_dim)
    o_ctx = _attention(q, k_all, v_all, n_batch=b, s=s, t=s, kv_row0=n_req * t, **common)
    o_lat = _attention(q, k_all, v_all, n_batch=n_req, s=dec_seq, t=t, q_row0=n_ctx, **common)
    state = (ckv[:n_ctx].reshape(b, s, kv_rank), kpe[:n_ctx].reshape(b, s, rope_dim))
    return jnp.concatenate([o_ctx, o_lat], axis=0), state


def kernel(x_prompt, x_sample, c, cache_k0, cache_v0, cache_k1, cache_v1, cache_ckv2, cache_kpe2, cache_k3, cache_v3, c_ctx, ada_w0, ada_b0, norm1_g0, norm2_g0, ffn_w13_0, ffn_w2_0, gqa_wqkv0, gqa_qg0, gqa_kg0, gqa_wo0, ada_w1, ada_b1, norm1_g1, norm2_g1, ffn_w13_1, ffn_w2_1, diff_wqkv1, diff_qg1, diff_kg1, diff_lq1_1, diff_lk1_1, diff_lq2_1, diff_lk2_1, diff_subln_g1, diff_wo1, ada_w2, ada_b2, norm1_g2, norm2_g2, ffn_w13_2, ffn_w2_2, mla_wdown2, mla_qa_g2, mla_wuq2, mla_kva_g2, mla_wukv2, mla_qg2, mla_kg2, mla_wo2, ada_w3, ada_b3, norm1_g3, norm2_g3, ffn_w13_3, ffn_w2_3, gqa_wqkv3, gqa_qg3, gqa_kg3, gqa_wo3):
    layers = [
        ((ada_w0, ada_b0, norm1_g0, norm2_g0, ffn_w13_0, ffn_w2_0),
         (gqa_wqkv0, gqa_qg0, gqa_kg0, gqa_wo0), (cache_k0, cache_v0)),
        ((ada_w1, ada_b1, norm1_g1, norm2_g1, ffn_w13_1, ffn_w2_1),
         (diff_wqkv1, diff_qg1, diff_kg1, diff_lq1_1, diff_lk1_1, diff_lq2_1, diff_lk2_1, diff_subln_g1, diff_wo1),
         (cache_k1, cache_v1)),
        ((ada_w2, ada_b2, norm1_g2, norm2_g2, ffn_w13_2, ffn_w2_2),
         (mla_wdown2, mla_qa_g2, mla_wuq2, mla_kva_g2, mla_wukv2, mla_qg2, mla_kg2, mla_wo2),
         (cache_ckv2, cache_kpe2)),
        ((ada_w3, ada_b3, norm1_g3, norm2_g3, ffn_w13_3, ffn_w2_3),
         (gqa_wqkv3, gqa_qg3, gqa_kg3, gqa_wo3), (cache_k3, cache_v3)),
    ]
    b, s, d = x_prompt.shape
    n_req, dec_seq, _ = x_sample.shape
    assert dec_seq % GRID_W == 0
    geom = _Geom(b * s, n_req, dec_seq)
    dims = (b, s, n_req, dec_seq)
    n_groups = 1 + n_req
    assert n_groups <= SUBLANES

    x = jnp.concatenate([x_prompt.reshape(b * s, d), x_sample.reshape(n_req * dec_seq, d)], axis=0)
    cs = jnp.concatenate([c_ctx[None, :], c, jnp.zeros((SUBLANES - n_groups, d), F32)], axis=0)

    new_state = []
    for l, (common, mixer, cache) in enumerate(layers):
        ada_w, ada_b, g1, g2, w13, w2 = common
        mod = _modulation(cs, ada_w, ada_b)
        mod3 = mod[:n_groups].reshape(n_groups * N_MOD, 1, d)
        h = _norm_modulate(geom, x, 0, d, g1, mod3, 0, 1, BF16)
        kind = l % N_MIXERS
        if kind == 0:
            o, st = _gqa_mixer(geom, dims, h, mixer, cache)
        elif kind == 1:
            o, st = _diff_mixer(geom, dims, h, mixer, cache, 0.8 - 0.6 * math.exp(-0.3 * l))
        else:
            o, st = _mla_mixer(geom, dims, h, mixer, cache)
        new_state.extend(st)
        x = _matmul_gated_residual(geom, o, mixer[-1], x, mod3, 2)
        h = _norm_modulate(geom, x, 0, d, g2, mod3, 3, 4, BF16)
        hidden = _matmul_swiglu(h, w13)
        x = _matmul_gated_residual(geom, hidden, w2.astype(BF16), x, mod3, 5, row_target=ROW_TILE // 2)

    y_prompt = x[:b * s].reshape(b, s, d)
    y_sample = x[b * s:].reshape(n_req, dec_seq, d)
    return (y_prompt, y_sample, *new_state)
```

```python
import functools
import math

import jax
import jax.numpy as jnp
from jax import lax
from jax.experimental import pallas as pl
from jax.experimental.pallas import tpu as pltpu

F32 = jnp.float32
BF16 = jnp.bfloat16

EPS = 1e-6
ROPE_THETA = 10000.0
GRID_W = 64
N_MOD = 6
N_MIXERS = 3
LANES = 128
SUBLANES = 8
VMEM_LIMIT_BYTES = 56 * 1024 * 1024
NEG_BIG = -1e30

ROW_TILE = 1024
COL_TILE = 512
SWIGLU_TILE = 256
NORM_ROWS = 256
MOD_COLS = 1024
KV_CHUNK = 512
Q_ROWS_GROUPED = 2048
Q_ROWS_SINGLE = 4096
Q_ROWS_DIFF = 2048
SOFTMAX_BLOCK_ELEMS = 16 * SUBLANES * LANES
HEAD_SUB_BLOCKS = 4


def _params(*sem):
    return pltpu.CompilerParams(dimension_semantics=sem, vmem_limit_bytes=VMEM_LIMIT_BYTES)


def _tile(target, n, *also):
    t = min(target, n)
    while any(d % t for d in (n,) + also):
        t //= 2
    assert t >= 1
    return t


def _kv_chunk(t):
    for cand in range(min(KV_CHUNK, t) // LANES * LANES, 0, -LANES):
        if t % cand == 0:
            return cand
    return t


class _Geom:
    def __init__(self, n_ctx, n_req, dec_seq):
        self.n_ctx, self.n_req, self.dec_seq = n_ctx, n_req, dec_seq
        self.m = n_ctx + n_req * dec_seq

    def row_tile(self, target):
        return _tile(target, self.n_ctx, self.dec_seq)

    def group_of_block(self, i, bm):
        nc, per = self.n_ctx // bm, self.dec_seq // bm
        return jnp.where(i < nc, 0, 1 + (i - nc) // per)

    def rope_block(self, i, bm):
        nc, per = self.n_ctx // bm, self.dec_seq // bm
        return jnp.where(i < nc, 0, 1 + (i - nc) % per)


def _rope_cos_sin(n_tokens, rot_dim):
    rows = n_tokens // GRID_W
    row_pos = jnp.repeat(jnp.arange(rows, dtype=F32), GRID_W)
    col_pos = jnp.tile(jnp.arange(GRID_W, dtype=F32), rows)
    half = rot_dim // 2
    inv = ROPE_THETA ** (-jnp.arange(0, half, 2, dtype=F32) / half)
    ang_r = row_pos[:, None] * inv
    ang_c = col_pos[:, None] * inv
    cos = jnp.concatenate([jnp.cos(ang_r), jnp.cos(ang_r), jnp.cos(ang_c), jnp.cos(ang_c)], axis=-1)
    sin = jnp.concatenate([-jnp.sin(ang_r), jnp.sin(ang_r), -jnp.sin(ang_c), jnp.sin(ang_c)], axis=-1)
    return cos, sin


def _rope_table(n_tokens, rot_dim, width, lane0, bm):
    cos, sin = _rope_cos_sin(n_tokens, rot_dim)
    cos_full = jnp.ones((n_tokens, width), F32).at[:, lane0:lane0 + rot_dim].set(cos)
    sin_full = jnp.zeros((n_tokens, width), F32).at[:, lane0:lane0 + rot_dim].set(sin)
    cos_tab = jnp.concatenate([jnp.ones((bm, width), F32), cos_full], axis=0)
    sin_tab = jnp.concatenate([jnp.zeros((bm, width), F32), sin_full], axis=0)
    return cos_tab, sin_tab


def _rope_slab(y, cos, sin, rb):
    lane = lax.broadcasted_iota(jnp.int32, y.shape, 1)
    first = ((lane // rb) % 2) == 0
    partner = jnp.where(first, pltpu.roll(y, LANES - rb, 1), pltpu.roll(y, rb, 1))
    return y * cos + partner * sin


def _mod_kernel(c_ref, w_ref, b_ref, o_ref):
    a = jax.nn.silu(c_ref[...]).astype(BF16)
    o_ref[...] = jnp.dot(a, w_ref[...].astype(BF16), preferred_element_type=F32) + b_ref[...]


def _modulation(cs, w, b):
    d, n = w.shape
    bn = _tile(MOD_COLS, n)
    return pl.pallas_call(
        _mod_kernel,
        grid=(n // bn,),
        in_specs=[pl.BlockSpec((SUBLANES, d), lambda j: (0, 0)),
                  pl.BlockSpec((d, bn), lambda j: (0, j)),
                  pl.BlockSpec((1, bn), lambda j: (0, j))],
        out_specs=pl.BlockSpec((SUBLANES, bn), lambda j: (0, j)),
        out_shape=jax.ShapeDtypeStruct((SUBLANES, n), F32),
        compiler_params=_params("arbitrary"),
        name="adaln_modulation",
    )(cs, w, b.reshape(1, n))


def _norm_kernel(x_ref, g_ref, sh_ref, sc_ref, o_ref):
    x = x_ref[...]
    ms = jnp.mean(x * x, axis=-1, keepdims=True)
    y = x * lax.rsqrt(ms + EPS) * g_ref[...]
    o_ref[...] = (y * (1.0 + sc_ref[...]) + sh_ref[...]).astype(o_ref.dtype)


def _norm_modulate(geom, x, col_block, width, g, mod3, shift_kind, scale_kind, out_dtype):
    m = x.shape[0]
    bm = geom.row_tile(NORM_ROWS)

    def mod_spec(kind):
        return pl.BlockSpec((None, 1, width), lambda i: (geom.group_of_block(i, bm) * N_MOD + kind, 0, 0))

    return pl.pallas_call(
        _norm_kernel,
        grid=(m // bm,),
        in_specs=[pl.BlockSpec((bm, width), lambda i: (i, col_block)),
                  pl.BlockSpec((1, width), lambda i: (0, 0)),
                  mod_spec(shift_kind), mod_spec(scale_kind)],
        out_specs=pl.BlockSpec((bm, width), lambda i: (i, 0)),
        out_shape=jax.ShapeDtypeStruct((m, width), out_dtype),
        compiler_params=_params("arbitrary"),
        name="rmsnorm_modulate",
    )(x, g.reshape(1, width), mod3, mod3)


def _plain_norm(geom, x, col_block, width, g, out_dtype):
    zeros = jnp.zeros((N_MOD * (1 + geom.n_req), 1, width), F32)
    return _norm_modulate(geom, x, col_block, width, g, zeros, 0, 1, out_dtype)


def _matmul_call(a, weights, extras, extra_specs, out_shapes, out_specs, epilogue, *, bm, bn, n_cols, name,
                 n_sub=1, row_block0=0, n_rows=None):
    k = a.shape[1]
    m = a.shape[0] if n_rows is None else n_rows
    n_w, n_ex, n_out = len(weights), len(extras), len(out_shapes)
    staged = n_w > 1 or weights[0][0].dtype != BF16

    def kernel(*refs):
        a_ref = refs[0]
        w_refs = refs[1:1 + n_w]
        ex_refs = refs[1 + n_w:1 + n_w + n_ex]
        out_refs = refs[1 + n_w + n_ex:1 + n_w + n_ex + n_out]
        if staged:
            w_bf16 = refs[-1]

            @pl.when(pl.program_id(1) == 0)
            def _():
                for idx, w_ref in enumerate(w_refs):
                    w_bf16[:, idx * bn:(idx + 1) * bn] = w_ref[...].astype(BF16)

            w_val = w_bf16[...]
        else:
            w_val = w_refs[0][...]
        sub = bm // n_sub
        for r in range(n_sub):
            rows = pl.ds(r * sub, sub)
            acc = jnp.dot(a_ref[rows, :], w_val, preferred_element_type=F32)
            epilogue([acc[:, idx * bn:(idx + 1) * bn] for idx in range(n_w)], ex_refs, out_refs, rows)

    in_specs = [pl.BlockSpec((bm, k), lambda j, i: (row_block0 + i, 0))]
    for _, off in weights:
        in_specs.append(pl.BlockSpec((k, bn), lambda j, i, off=off: (0, off + j)))
    in_specs += list(extra_specs)
    return pl.pallas_call(
        kernel,
        grid=(n_cols // bn, m // bm),
        in_specs=in_specs,
        out_specs=out_specs,
        out_shape=out_shapes,
        scratch_shapes=[pltpu.VMEM((k, n_w * bn), BF16)] if staged else [],
        compiler_params=_params("arbitrary", "arbitrary"),
        name=name,
    )(a, *[w for w, _ in weights], *extras)


def _matmul_plain(a, w, out_dtype):
    m, n = a.shape[0], w.shape[1]
    bm, bn = _tile(ROW_TILE, m), _tile(COL_TILE, n)

    def epilogue(accs, ex, outs, rows):
        outs[0][rows, :] = accs[0].astype(out_dtype)

    return _matmul_call(a, [(w, 0)], [], [], [jax.ShapeDtypeStruct((m, n), out_dtype)],
                        [pl.BlockSpec((bm, bn), lambda j, i: (i, j))], epilogue,
                        bm=bm, bn=bn, n_cols=n, name="matmul")[0]


def _matmul_gated_residual(geom, a, w, res, mod3, gate_kind, row_target=ROW_TILE):
    m, n = a.shape[0], w.shape[1]
    bm, bn = geom.row_tile(row_target), _tile(COL_TILE, n)

    def epilogue(accs, ex, outs, rows):
        res_ref, gate_ref = ex
        outs[0][rows, :] = res_ref[rows, :] + gate_ref[...] * accs[0]

    extra_specs = [pl.BlockSpec((bm, bn), lambda j, i: (i, j)),
                   pl.BlockSpec((None, 1, bn), lambda j, i: (geom.group_of_block(i, bm) * N_MOD + gate_kind, 0, j))]
    return _matmul_call(a, [(w, 0)], [res, mod3], extra_specs, [jax.ShapeDtypeStruct((m, n), F32)],
                        [pl.BlockSpec((bm, bn), lambda j, i: (i, j))], epilogue,
                        bm=bm, bn=bn, n_cols=n, name="matmul_gated_residual")[0]


def _matmul_swiglu(a, w13):
    m, f = a.shape[0], w13.shape[1] // 2
    bm, bn = _tile(ROW_TILE, m), _tile(SWIGLU_TILE, f)

    def epilogue(accs, ex, outs, rows):
        outs[0][rows, :] = (jax.nn.silu(accs[0]) * accs[1]).astype(BF16)

    return _matmul_call(a, [(w13, 0), (w13, f // bn)], [], [], [jax.ShapeDtypeStruct((m, f), BF16)],
                        [pl.BlockSpec((bm, bn), lambda j, i: (i, j))], epilogue,
                        bm=bm, bn=bn, n_cols=f, name="matmul_swiglu")[0]


def _matmul_heads(geom, a, w, col0, n_cols, gain_cols, cos_tab, sin_tab, *, head_width, n_valid, rb,
                  rope_slabs, bm, normed=True, row0=0, n_rows=None, with_f32=False):
    n_rows = a.shape[0] if n_rows is None else n_rows
    bn = _tile(COL_TILE, n_cols, col0)
    assert bn % head_width == 0 and head_width % LANES == 0 and row0 % bm == 0
    rb0 = row0 // bm

    def epilogue(accs, ex, outs, rows):
        acc = accs[0]
        gain_ref, cos_ref, sin_ref = ex
        for h in range(bn // head_width):
            lo = h * head_width
            y = acc[:, lo:lo + head_width]
            if normed:
                ms = jnp.sum(y * y, axis=-1, keepdims=True) / n_valid
                y = y * lax.rsqrt(ms + EPS) * gain_ref[:, lo:lo + head_width]
            for s in range(head_width // LANES):
                ys = y[:, s * LANES:(s + 1) * LANES]
                if normed and s in rope_slabs:
                    ys = _rope_slab(ys, cos_ref[rows, s * LANES:(s + 1) * LANES],
                                    sin_ref[rows, s * LANES:(s + 1) * LANES], rb)
                cols = slice(lo + s * LANES, lo + (s + 1) * LANES)
                outs[0][rows, cols] = ys.astype(BF16)
                if with_f32:
                    outs[1][rows, cols] = ys

    extra_specs = [pl.BlockSpec((1, bn), lambda j, i: (0, j)),
                   pl.BlockSpec((bm, head_width), lambda j, i: (geom.rope_block(rb0 + i, bm), 0)),
                   pl.BlockSpec((bm, head_width), lambda j, i: (geom.rope_block(rb0 + i, bm), 0))]
    out_shapes = [jax.ShapeDtypeStruct((n_rows, n_cols), BF16)]
    out_specs = [pl.BlockSpec((bm, bn), lambda j, i: (i, j))]
    if with_f32:
        out_shapes.append(jax.ShapeDtypeStruct((n_rows, n_cols), F32))
        out_specs.append(pl.BlockSpec((bm, bn), lambda j, i: (i, j)))
    return _matmul_call(a, [(w, col0 // bn)], [gain_cols, cos_tab, sin_tab], extra_specs,
                        out_shapes, out_specs, epilogue, bm=bm, bn=bn, n_cols=n_cols,
                        row_block0=rb0, n_rows=n_rows, n_sub=HEAD_SUB_BLOCKS, name="matmul_head_norm_rope")


def _matmul_mla_kv(ckv, w, kpe, cos_tab, sin_tab, kg_nope, kg_rope, rope_block, *, bm, rb, n_valid):
    m, n = ckv.shape[0], w.shape[1]
    hw = 2 * LANES
    bn = _tile(COL_TILE, n)
    assert bn % hw == 0

    def epilogue(accs, ex, outs, rows):
        acc = accs[0]
        kpe_ref, cos_ref, sin_ref, gn_ref, gr_ref = ex
        k_ref, v_ref = outs
        pe = kpe_ref[rows, :]
        ss_pe = jnp.sum(pe * pe, axis=-1, keepdims=True)
        pe_rot = _rope_slab(pe * gr_ref[...], cos_ref[rows, :], sin_ref[rows, :], rb)
        for h in range(bn // hw):
            kn = acc[:, h * hw:h * hw + LANES]
            ms = (jnp.sum(kn * kn, axis=-1, keepdims=True) + ss_pe) / n_valid
            r = lax.rsqrt(ms + EPS)
            k_ref[rows, h * hw:h * hw + LANES] = (kn * r * gn_ref[...]).astype(BF16)
            k_ref[rows, h * hw + LANES:(h + 1) * hw] = (pe_rot * r).astype(BF16)
            v_ref[rows, h * LANES:(h + 1) * LANES] = acc[:, h * hw + LANES:(h + 1) * hw].astype(BF16)

    extra_specs = [pl.BlockSpec((bm, LANES), lambda j, i: (i, 0)),
                   pl.BlockSpec((bm, LANES), lambda j, i: (rope_block(i), 0)),
                   pl.BlockSpec((bm, LANES), lambda j, i: (rope_block(i), 0)),
                   pl.BlockSpec((1, LANES), lambda j, i: (0, 0)),
                   pl.BlockSpec((1, LANES), lambda j, i: (0, 0))]
    return _matmul_call(ckv, [(w, 0)], [kpe, cos_tab, sin_tab, kg_nope, kg_rope], extra_specs,
                        [jax.ShapeDtypeStruct((m, n), BF16), jax.ShapeDtypeStruct((m, n // 2), BF16)],
                        [pl.BlockSpec((bm, bn), lambda j, i: (i, j)),
                         pl.BlockSpec((bm, bn // 2), lambda j, i: (i, j))], epilogue,
                        bm=bm, bn=bn, n_cols=n, name="matmul_mla_kv")


class _Softmax:
    N_SCRATCH = 6

    def __init__(self, s, p, m, l, alpha, acc):
        self.s, self.p, self.m, self.l, self.alpha, self.acc = s, p, m, l, alpha, acc

    @staticmethod
    def scratch(rows, tk, dv):
        assert tk % LANES == 0 and dv % LANES == 0
        return [pltpu.VMEM((rows, tk), F32), pltpu.VMEM((rows, tk), BF16),
                pltpu.VMEM((rows, LANES), F32), pltpu.VMEM((rows, LANES), F32), pltpu.VMEM((rows, LANES), F32),
                pltpu.VMEM((rows, dv), F32)]

    def start(self):
        self.m[...] = jnp.full(self.m.shape, NEG_BIG, F32)
        self.l[...] = jnp.zeros(self.l.shape, F32)
        self.acc[...] = jnp.zeros(self.acc.shape, F32)

    def scores(self, q, k):
        self.s[...] = lax.dot_general(q, k, (((1,), (1,)), ((), ())), preferred_element_type=F32)

    def update(self):
        rows, tk = self.s.shape
        rb = _tile(max(SUBLANES, 1 << ((SOFTMAX_BLOCK_ELEMS // tk).bit_length() - 1)), rows)
        blocks = [pl.ds(r * rb, rb) for r in range(rows // rb)]
        slabs = [pl.ds(j * LANES, LANES) for j in range(tk // LANES)]
        for sl in blocks:
            mx = self.s[sl, slabs[0]]
            for cols in slabs[1:]:
                mx = jnp.maximum(mx, self.s[sl, cols])
            m_old = self.m[sl, :]
            m_new = jnp.maximum(m_old, jnp.broadcast_to(jnp.max(mx, axis=1, keepdims=True), m_old.shape))
            self.alpha[sl, :] = jnp.exp(m_old - m_new)
            self.m[sl, :] = m_new
        for sl in blocks:
            m = self.m[sl, :]
            total = None
            for cols in slabs:
                p = jnp.exp(self.s[sl, cols] - m)
                total = p if total is None else total + p
                self.p[sl, cols] = p.astype(BF16)
            self.l[sl, :] = self.alpha[sl, :] * self.l[sl, :] + total

    def accumulate(self, v):
        pv = jnp.dot(self.p[...], v, preferred_element_type=F32)
        alpha = self.alpha[...]
        for j in range(self.acc.shape[1] // LANES):
            cols = pl.ds(j * LANES, LANES)
            self.acc[:, cols] = alpha * self.acc[:, cols] + pv[:, j * LANES:(j + 1) * LANES]

    @staticmethod
    def chunk(maps, queries, keys, v):
        for sm, q, k in zip(maps, queries, keys):
            sm.scores(q, k)
        for sm in maps:
            sm.update()
        for sm in maps:
            sm.accumulate(v)

    def result(self, lo, n):
        inv = 1.0 / jnp.sum(self.l[lo:lo + n, :], axis=1, keepdims=True)
        return self.acc[lo:lo + n, :] * inv


def _attn_kernel(q_ref, k_ref, v_ref, o_ref, *scratch, n_rep, dk, dv, tk):
    bq, t = q_ref.shape[0], k_ref.shape[0]
    sm = _Softmax(*scratch[:_Softmax.N_SCRATCH])
    if n_rep > 1:
        q_all = scratch[_Softmax.N_SCRATCH]
        for r in range(n_rep):
            q_all[r * bq:(r + 1) * bq, :] = q_ref[:, r * dk:(r + 1) * dk]
    else:
        q_all = q_ref
    sm.start()

    def body(c, carry):
        off = pl.multiple_of(c * tk, tk)
        _Softmax.chunk([sm], [q_all[...]], [k_ref[pl.ds(off, tk), :]], v_ref[pl.ds(off, tk), :])
        return carry

    lax.fori_loop(0, t // tk, body, 0)
    for r in range(n_rep):
        o_ref[:, r * dv:(r + 1) * dv] = sm.result(r * bq, bq).astype(o_ref.dtype)


def _attention(q2d, k2d, v2d, *, n_batch, s, t, n_groups, n_rep, dk, dv, q_row0=0, kv_row0=0):
    bq = _tile(Q_ROWS_GROUPED if n_rep > 1 else Q_ROWS_SINGLE, s)
    tk = _kv_chunk(t)
    qw, ow = n_rep * dk, n_rep * dv
    assert q_row0 % bq == 0 and kv_row0 % t == 0
    qb0, kb0 = q_row0 // bq, kv_row0 // t
    nq = s // bq
    rows = n_rep * bq
    scratch = _Softmax.scratch(rows, tk, dv)
    if n_rep > 1:
        scratch.append(pltpu.VMEM((rows, dk), BF16))
    kernel = functools.partial(_attn_kernel, n_rep=n_rep, dk=dk, dv=dv, tk=tk)
    return pl.pallas_call(
        kernel,
        grid=(n_batch, n_groups, nq),
        in_specs=[pl.BlockSpec((bq, qw), lambda b, g, i: (qb0 + b * nq + i, g)),
                  pl.BlockSpec((t, dk), lambda b, g, i: (kb0 + b, g)),
                  pl.BlockSpec((t, dv), lambda b, g, i: (kb0 + b, g))],
        out_specs=pl.BlockSpec((bq, ow), lambda b, g, i: (b * nq + i, g)),
        out_shape=jax.ShapeDtypeStruct((n_batch * s, n_groups * ow), BF16),
        scratch_shapes=scratch,
        compiler_params=_params("arbitrary", "arbitrary", "arbitrary"),
        name="attention",
    )(q2d, k2d, v2d)


def _diff_attn_kernel(q_ref, k_ref, v_ref, lq1_ref, lk1_ref, lq2_ref, lk2_ref, g_ref, o_ref, *scratch,
                      d, tk, lam_init):
    bq, t = q_ref.shape[0], k_ref.shape[0]
    maps = (_Softmax(*scratch[:_Softmax.N_SCRATCH]), _Softmax(*scratch[_Softmax.N_SCRATCH:]))
    for sm in maps:
        sm.start()

    def body(c, carry):
        off = pl.multiple_of(c * tk, tk)
        queries = [q_ref[:, i * d:(i + 1) * d] for i in range(2)]
        keys = [k_ref[pl.ds(off, tk), i * d:(i + 1) * d] for i in range(2)]
        _Softmax.chunk(maps, queries, keys, v_ref[pl.ds(off, tk), :])
        return carry

    lax.fori_loop(0, t // tk, body, 0)
    lam = (jnp.exp(jnp.sum(lq1_ref[...] * lk1_ref[...], keepdims=True))
           - jnp.exp(jnp.sum(lq2_ref[...] * lk2_ref[...], keepdims=True)) + lam_init)
    o = maps[0].result(0, bq) - lam * maps[1].result(0, bq)
    ms = jnp.mean(o * o, axis=-1, keepdims=True)
    o = o * lax.rsqrt(ms + EPS) * g_ref[...] * (1.0 - lam_init)
    o_ref[...] = o.astype(o_ref.dtype)


def _diff_attention(q2d, k2d, v2d, lams, subln_g, *, n_batch, s, t, n_heads, d, lam_init, q_row0=0, kv_row0=0):
    w = 2 * d
    bq = _tile(Q_ROWS_DIFF, s)
    tk = _kv_chunk(t)
    assert q_row0 % bq == 0 and kv_row0 % t == 0
    qb0, kb0 = q_row0 // bq, kv_row0 // t
    nq = s // bq
    kernel = functools.partial(_diff_attn_kernel, d=d, tk=tk, lam_init=lam_init)
    vec = pl.BlockSpec((1, d), lambda b, h, i: (0, 0))
    return pl.pallas_call(
        kernel,
        grid=(n_batch, n_heads, nq),
        in_specs=[pl.BlockSpec((bq, w), lambda b, h, i: (qb0 + b * nq + i, h)),
                  pl.BlockSpec((t, w), lambda b, h, i: (kb0 + b, h)),
                  pl.BlockSpec((t, w), lambda b, h, i: (kb0 + b, h)),
                  vec, vec, vec, vec,
                  pl.BlockSpec((1, w), lambda b, h, i: (0, 0))],
        out_specs=pl.BlockSpec((bq, w), lambda b, h, i: (b * nq + i, h)),
        out_shape=jax.ShapeDtypeStruct((n_batch * s, n_heads * w), BF16),
        scratch_shapes=_Softmax.scratch(bq, tk, w) + _Softmax.scratch(bq, tk, w),
        compiler_params=_params("arbitrary", "arbitrary", "arbitrary"),
        name="diff_attention",
    )(q2d, k2d, v2d, *[a.reshape(1, d) for a in lams], subln_g.reshape(1, w))


def _with_cache(cache2d, new2d, n_req, dec_seq):
    w = new2d.shape[-1]
    both = jnp.concatenate([cache2d.reshape(n_req, -1, w).astype(BF16), new2d.reshape(n_req, dec_seq, w)], axis=1)
    return both.reshape(-1, w)


def _qkv_attention_inputs(geom, dims, h, wqkv, qg, kg, q_cols, k_cols, v_cols, hd, cache):
    b, s, n_req, dec_seq = dims
    bm = geom.row_tile(ROW_TILE)
    cos_tab, sin_tab = _rope_table(dec_seq, hd, hd, 0, bm)
    heads = dict(head_width=hd, n_valid=hd, rb=hd // 4, rope_slabs=(0,), bm=bm)
    q_gain = (jnp.tile(qg, q_cols // hd) * hd ** -0.5).reshape(1, -1)
    k_gain = jnp.tile(kg, k_cols // hd).reshape(1, -1)
    v_gain = jnp.ones((1, v_cols), F32)
    ctx = dict(row0=0, n_rows=geom.n_ctx, with_f32=True)
    lat = dict(row0=geom.n_ctx, n_rows=n_req * dec_seq)
    k_args = (geom, h, wqkv, q_cols, k_cols, k_gain, cos_tab, sin_tab)
    v_args = (geom, h, wqkv, q_cols + k_cols, v_cols, v_gain, cos_tab, sin_tab)
    q, = _matmul_heads(geom, h, wqkv, 0, q_cols, q_gain, cos_tab, sin_tab, **heads)
    k, k_state = _matmul_heads(*k_args, **ctx, **heads)
    v, v_state = _matmul_heads(*v_args, normed=False, **ctx, **heads)
    k_lat, = _matmul_heads(*k_args, **lat, **heads)
    v_lat, = _matmul_heads(*v_args, normed=False, **lat, **heads)
    k_all = _with_cache(cache[0], k_lat, n_req, dec_seq)
    v_all = _with_cache(cache[1], v_lat, n_req, dec_seq)
    return q, k, v, k_all, v_all, k_state, v_state


def _gqa_mixer(geom, dims, h, p, cache):
    wqkv, qg, kg, _ = p
    b, s, n_req, dec_seq = dims
    d = wqkv.shape[0]
    hd = qg.shape[0]
    n_heads = d // hd
    n_kv = (wqkv.shape[1] - d) // (2 * hd)
    kv_w = n_kv * hd
    past = cache[0].shape[1]
    q, k, v, k_all, v_all, k_state, v_state = _qkv_attention_inputs(
        geom, dims, h, wqkv, qg, kg, d, kv_w, kv_w, hd, cache)
    common = dict(n_groups=n_kv, n_rep=n_heads // n_kv, dk=hd, dv=hd)
    o_ctx = _attention(q, k, v, n_batch=b, s=s, t=s, **common)
    o_lat = _attention(q, k_all, v_all, n_batch=n_req, s=dec_seq, t=past + dec_seq, q_row0=geom.n_ctx, **common)
    state = (k_state.reshape(b, s, n_kv, hd), v_state.reshape(b, s, n_kv, hd))
    return jnp.concatenate([o_ctx, o_lat], axis=0), state


def _diff_mixer(geom, dims, h, p, cache, lam_init):
    wqkv, qg, kg, lq1, lk1, lq2, lk2, subln_g, _ = p
    b, s, n_req, dec_seq = dims
    d = wqkv.shape[0]
    hd = qg.shape[0]
    n_heads = d // (2 * hd)
    past = cache[0].shape[1]
    q, k, v, k_all, v_all, k_state, v_state = _qkv_attention_inputs(
        geom, dims, h, wqkv, qg, kg, d, d, d, hd, cache)
    common = dict(n_heads=n_heads, d=hd, lam_init=lam_init)
    lams = (lq1, lk1, lq2, lk2)
    o_ctx = _diff_attention(q, k, v, lams, subln_g, n_batch=b, s=s, t=s, **common)
    o_lat = _diff_attention(q, k_all, v_all, lams, subln_g, n_batch=n_req, s=dec_seq, t=past + dec_seq,
                            q_row0=geom.n_ctx, **common)
    state = (k_state.reshape(b, s, n_heads, 2, hd), v_state.reshape(b, s, n_heads, 2 * hd))
    return jnp.concatenate([o_ctx, o_lat], axis=0), state


def _mla_mixer(geom, dims, h, p, cache):
    wdown, qa_g, wuq, kva_g, wukv, qg, kg, wo = p
    b, s, n_req, dec_seq = dims
    q_rank, kv_rank = wuq.shape[0], wukv.shape[0]
    rope_dim = wdown.shape[1] - q_rank - kv_rank
    qk_dim = qg.shape[0]
    nope = qk_dim - rope_dim
    v_dim = LANES
    n_heads = wo.shape[0] // v_dim
    assert nope == LANES and rope_dim <= LANES and wukv.shape[1] == n_heads * 2 * LANES
    assert q_rank % kv_rank == 0
    hw = 2 * LANES
    past = cache[0].shape[1]
    t = past + dec_seq
    n_ctx = geom.n_ctx

    n_down = wdown.shape[1]
    pad_cols = -n_down % COL_TILE
    down = _matmul_plain(h, jnp.pad(wdown, ((0, 0), (0, pad_cols))), F32)
    cq = _plain_norm(geom, down, 0, q_rank, qa_g, BF16)
    ckv = _plain_norm(geom, down, q_rank // kv_rank, kv_rank, kva_g, F32)
    kpe = down[:, q_rank + kv_rank:n_down]

    wuq_p = jnp.pad(wuq.reshape(q_rank, n_heads, qk_dim), ((0, 0), (0, 0), (0, hw - qk_dim))).reshape(q_rank, n_heads * hw)
    gq = (jnp.tile(jnp.pad(qg, (0, hw - qk_dim)), n_heads) * qk_dim ** -0.5).reshape(1, -1)
    bm = geom.row_tile(ROW_TILE)
    cos_q, sin_q = _rope_table(dec_seq, rope_dim, hw, nope, bm)
    q, = _matmul_heads(geom, cq, wuq_p, 0, n_heads * hw, gq, cos_q, sin_q, head_width=hw, n_valid=qk_dim,
                       rb=rope_dim // 4, rope_slabs=(1,), bm=bm)

    def rows(cached, new, width):
        lat = jnp.concatenate([cached.reshape(n_req, past, width), new[n_ctx:].reshape(n_req, dec_seq, width)], axis=1)
        return jnp.concatenate([lat.reshape(n_req * t, width), new[:n_ctx]], axis=0)

    ckv_all = rows(cache[0], ckv, kv_rank).astype(BF16)
    kpe_all = jnp.pad(rows(cache[1], kpe, rope_dim), ((0, 0), (0, LANES - rope_dim)))
    bmk = _tile(COL_TILE, past, dec_seq, n_ctx)
    per_req, past_blocks = t // bmk, past // bmk

    def rope_block(i):
        r = i % per_req
        return jnp.where((i < n_req * per_req) & (r >= past_blocks), 1 + r - past_blocks, 0)

    cos_k, sin_k = _rope_table(dec_seq, rope_dim, LANES, 0, bmk)
    kg_nope = kg[:nope].reshape(1, LANES)
    kg_rope = jnp.pad(kg[nope:], (0, LANES - rope_dim)).reshape(1, LANES)
    k_all, v_all = _matmul_mla_kv(ckv_all, wukv, kpe_all, cos_k, sin_k, kg_nope, kg_rope, rope_block,
                                  bm=bmk, rb=rope_dim // 4, n_valid=qk_dim)

    common = dict(n_groups=n_heads, n_rep=1, dk=hw, dv=v_dim)
    o_ctx = _attention(q, k_all, v_all, n_batch=b, s=s, t=s, kv_row0=n_req * t, **common)
    o_lat = _attention(q, k_all, v_all, n_batch=n_req, s=dec_seq, t=t, q_row0=n_ctx, **common)
    state = (ckv[:n_ctx].reshape(b, s, kv_rank), kpe[:n_ctx].reshape(b, s, rope_dim))
    return jnp.concatenate([o_ctx, o_lat], axis=0), state


def kernel(x_prompt, x_sample, c, cache_k0, cache_v0, cache_k1, cache_v1, cache_ckv2, cache_kpe2, cache_k3, cache_v3, c_ctx, ada_w0, ada_b0, norm1_g0, norm2_g0, ffn_w13_0, ffn_w2_0, gqa_wqkv0, gqa_qg0, gqa_kg0, gqa_wo0, ada_w1, ada_b1, norm1_g1, norm2_g1, ffn_w13_1, ffn_w2_1, diff_wqkv1, diff_qg1, diff_kg1, diff_lq1_1, diff_lk1_1, diff_lq2_1, diff_lk2_1, diff_subln_g1, diff_wo1, ada_w2, ada_b2, norm1_g2, norm2_g2, ffn_w13_2, ffn_w2_2, mla_wdown2, mla_qa_g2, mla_wuq2, mla_kva_g2, mla_wukv2, mla_qg2, mla_kg2, mla_wo2, ada_w3, ada_b3, norm1_g3, norm2_g3, ffn_w13_3, ffn_w2_3, gqa_wqkv3, gqa_qg3, gqa_kg3, gqa_wo3):
    layers = [
        ((ada_w0, ada_b0, norm1_g0, norm2_g0, ffn_w13_0, ffn_w2_0),
         (gqa_wqkv0, gqa_qg0, gqa_kg0, gqa_wo0), (cache_k0, cache_v0)),
        ((ada_w1, ada_b1, norm1_g1, norm2_g1, ffn_w13_1, ffn_w2_1),
         (diff_wqkv1, diff_qg1, diff_kg1, diff_lq1_1, diff_lk1_1, diff_lq2_1, diff_lk2_1, diff_subln_g1, diff_wo1),
         (cache_k1, cache_v1)),
        ((ada_w2, ada_b2, norm1_g2, norm2_g2, ffn_w13_2, ffn_w2_2),
         (mla_wdown2, mla_qa_g2, mla_wuq2, mla_kva_g2, mla_wukv2, mla_qg2, mla_kg2, mla_wo2),
         (cache_ckv2, cache_kpe2)),
        ((ada_w3, ada_b3, norm1_g3, norm2_g3, ffn_w13_3, ffn_w2_3),
         (gqa_wqkv3, gqa_qg3, gqa_kg3, gqa_wo3), (cache_k3, cache_v3)),
    ]
    b, s, d = x_prompt.shape
    n_req, dec_seq, _ = x_sample.shape
    assert dec_seq % GRID_W == 0
    geom = _Geom(b * s, n_req, dec_seq)
    dims = (b, s, n_req, dec_seq)
    n_groups = 1 + n_req
    assert n_groups <= SUBLANES

    x = jnp.concatenate([x_prompt.reshape(b * s, d), x_sample.reshape(n_req * dec_seq, d)], axis=0)
    cs = jnp.concatenate([c_ctx[None, :], c, jnp.zeros((SUBLANES - n_groups, d), F32)], axis=0)

    new_state = []
    for l, (common, mixer, cache) in enumerate(layers):
        ada_w, ada_b, g1, g2, w13, w2 = common
        mod = _modulation(cs, ada_w, ada_b)
        mod3 = mod[:n_groups].reshape(n_groups * N_MOD, 1, d)
        h = _norm_modulate(geom, x, 0, d, g1, mod3, 0, 1, BF16)
        kind = l % N_MIXERS
        if kind == 0:
            o, st = _gqa_mixer(geom, dims, h, mixer, cache)
        elif kind == 1:
            o, st = _diff_mixer(geom, dims, h, mixer, cache, 0.8 - 0.6 * math.exp(-0.3 * l))
        else:
            o, st = _mla_mixer(geom, dims, h, mixer, cache)
        new_state.extend(st)
        x = _matmul_gated_residual(geom, o, mixer[-1], x, mod3, 2)
        h = _norm_modulate(geom, x, 0, d, g2, mod3, 3, 4, BF16)
        hidden = _matmul_swiglu(h, w13)
        x = _matmul_gated_residual(geom, hidden, w2.astype(BF16), x, mod3, 5, row_target=ROW_TILE // 2)

    y_prompt = x[:b * s].reshape(b, s, d)
    y_sample = x[b * s:].reshape(n_req, dec_seq, d)
    return (y_prompt, y_sample, *new_state)
```

```python
import functools
import math

import jax
import jax.numpy as jnp
from jax import lax
from jax.experimental import pallas as pl
from jax.experimental.pallas import tpu as pltpu

F32 = jnp.float32
BF16 = jnp.bfloat16

EPS = 1e-6
ROPE_THETA = 10000.0
GRID_W = 64
N_MOD = 6
N_MIXERS = 3
LANES = 128
SUBLANES = 8
VMEM_LIMIT_BYTES = 56 * 1024 * 1024
NEG_BIG = -1e30
LOG2_E = math.log2(math.e)

ROW_TILE = 1024
COL_TILE = 512
SWIGLU_TILE = 256
NORM_ROWS = 256
MOD_COLS = 1024
KV_CHUNK = 512
Q_ROWS_GROUPED = 2048
Q_ROWS_SINGLE = 4096
Q_ROWS_DIFF = 2048
SOFTMAX_BLOCK_ELEMS = 16 * SUBLANES * LANES
HEAD_SUB_BLOCKS = 4
SHORT_SEQ_HEADS = 8


def _params(*sem):
    return pltpu.CompilerParams(dimension_semantics=sem, vmem_limit_bytes=VMEM_LIMIT_BYTES)


def _tile(target, n, *also):
    t = min(target, n)
    while any(d % t for d in (n,) + also):
        t //= 2
    assert t >= 1
    return t


def _kv_chunk(t):
    for cand in range(min(KV_CHUNK, t) // LANES * LANES, 0, -LANES):
        if t % cand == 0:
            return cand
    return t


class _Geom:
    def __init__(self, n_ctx, n_req, dec_seq):
        self.n_ctx, self.n_req, self.dec_seq = n_ctx, n_req, dec_seq
        self.m = n_ctx + n_req * dec_seq

    def row_tile(self, target):
        return _tile(target, self.n_ctx, self.dec_seq)

    def group_of_block(self, i, bm):
        nc, per = self.n_ctx // bm, self.dec_seq // bm
        return jnp.where(i < nc, 0, 1 + (i - nc) // per)

    def rope_block(self, i, bm):
        nc, per = self.n_ctx // bm, self.dec_seq // bm
        return jnp.where(i < nc, 0, 1 + (i - nc) % per)


def _rope_cos_sin(n_tokens, rot_dim):
    rows = n_tokens // GRID_W
    row_pos = jnp.repeat(jnp.arange(rows, dtype=F32), GRID_W)
    col_pos = jnp.tile(jnp.arange(GRID_W, dtype=F32), rows)
    half = rot_dim // 2
    inv = ROPE_THETA ** (-jnp.arange(0, half, 2, dtype=F32) / half)
    ang_r = row_pos[:, None] * inv
    ang_c = col_pos[:, None] * inv
    cos = jnp.concatenate([jnp.cos(ang_r), jnp.cos(ang_r), jnp.cos(ang_c), jnp.cos(ang_c)], axis=-1)
    sin = jnp.concatenate([-jnp.sin(ang_r), jnp.sin(ang_r), -jnp.sin(ang_c), jnp.sin(ang_c)], axis=-1)
    return cos, sin


def _rope_table(n_tokens, rot_dim, width, lane0, bm):
    cos, sin = _rope_cos_sin(n_tokens, rot_dim)
    cos_full = jnp.ones((n_tokens, width), F32).at[:, lane0:lane0 + rot_dim].set(cos)
    sin_full = jnp.zeros((n_tokens, width), F32).at[:, lane0:lane0 + rot_dim].set(sin)
    cos_tab = jnp.concatenate([jnp.ones((bm, width), F32), cos_full], axis=0)
    sin_tab = jnp.concatenate([jnp.zeros((bm, width), F32), sin_full], axis=0)
    return cos_tab, sin_tab


def _rope_slab(y, cos, sin, rb):
    lane = lax.broadcasted_iota(jnp.int32, y.shape, 1)
    first = ((lane // rb) % 2) == 0
    partner = jnp.where(first, pltpu.roll(y, LANES - rb, 1), pltpu.roll(y, rb, 1))
    return y * cos + partner * sin


def _mod_kernel(c_ref, w_ref, b_ref, o_ref):
    a = jax.nn.silu(c_ref[...]).astype(BF16)
    o_ref[...] = jnp.dot(a, w_ref[...].astype(BF16), preferred_element_type=F32) + b_ref[...]


def _modulation(cs, w, b):
    d, n = w.shape
    bn = _tile(MOD_COLS, n)
    return pl.pallas_call(
        _mod_kernel,
        grid=(n // bn,),
        in_specs=[pl.BlockSpec((SUBLANES, d), lambda j: (0, 0)),
                  pl.BlockSpec((d, bn), lambda j: (0, j)),
                  pl.BlockSpec((1, bn), lambda j: (0, j))],
        out_specs=pl.BlockSpec((SUBLANES, bn), lambda j: (0, j)),
        out_shape=jax.ShapeDtypeStruct((SUBLANES, n), F32),
        compiler_params=_params("arbitrary"),
        name="adaln_modulation",
    )(cs, w, b.reshape(1, n))


def _norm_kernel(x_ref, g_ref, sh_ref, sc_ref, o_ref):
    x = x_ref[...]
    ms = jnp.mean(x * x, axis=-1, keepdims=True)
    y = x * lax.rsqrt(ms + EPS) * g_ref[...]
    o_ref[...] = (y * (1.0 + sc_ref[...]) + sh_ref[...]).astype(o_ref.dtype)


def _norm_modulate(geom, x, col_block, width, g, mod3, shift_kind, scale_kind, out_dtype):
    m = x.shape[0]
    bm = geom.row_tile(NORM_ROWS)

    def mod_spec(kind):
        return pl.BlockSpec((None, 1, width), lambda i: (geom.group_of_block(i, bm) * N_MOD + kind, 0, 0))

    return pl.pallas_call(
        _norm_kernel,
        grid=(m // bm,),
        in_specs=[pl.BlockSpec((bm, width), lambda i: (i, col_block)),
                  pl.BlockSpec((1, width), lambda i: (0, 0)),
                  mod_spec(shift_kind), mod_spec(scale_kind)],
        out_specs=pl.BlockSpec((bm, width), lambda i: (i, 0)),
        out_shape=jax.ShapeDtypeStruct((m, width), out_dtype),
        compiler_params=_params("arbitrary"),
        name="rmsnorm_modulate",
    )(x, g.reshape(1, width), mod3, mod3)


def _plain_norm(geom, x, col_block, width, g, out_dtype):
    zeros = jnp.zeros((N_MOD * (1 + geom.n_req), 1, width), F32)
    return _norm_modulate(geom, x, col_block, width, g, zeros, 0, 1, out_dtype)


def _matmul_call(a, weights, extras, extra_specs, out_shapes, out_specs, epilogue, *, bm, bn, n_cols, name,
                 n_sub=1, row_block0=0, n_rows=None):
    k = a.shape[1]
    m = a.shape[0] if n_rows is None else n_rows
    n_w, n_ex, n_out = len(weights), len(extras), len(out_shapes)
    staged = n_w > 1 or weights[0][0].dtype != BF16

    def kernel(*refs):
        a_ref = refs[0]
        w_refs = refs[1:1 + n_w]
        ex_refs = refs[1 + n_w:1 + n_w + n_ex]
        out_refs = refs[1 + n_w + n_ex:1 + n_w + n_ex + n_out]
        if staged:
            w_bf16 = refs[-1]

            @pl.when(pl.program_id(1) == 0)
            def _():
                for idx, w_ref in enumerate(w_refs):
                    w_bf16[:, idx * bn:(idx + 1) * bn] = w_ref[...].astype(BF16)

            w_val = w_bf16[...]
        else:
            w_val = w_refs[0][...]
        sub = bm // n_sub
        for r in range(n_sub):
            rows = pl.ds(r * sub, sub)
            acc = jnp.dot(a_ref[rows, :], w_val, preferred_element_type=F32)
            epilogue([acc[:, idx * bn:(idx + 1) * bn] for idx in range(n_w)], ex_refs, out_refs, rows)

    in_specs = [pl.BlockSpec((bm, k), lambda j, i: (row_block0 + i, 0))]
    for _, off in weights:
        in_specs.append(pl.BlockSpec((k, bn), lambda j, i, off=off: (0, off + j)))
    in_specs += list(extra_specs)
    return pl.pallas_call(
        kernel,
        grid=(n_cols // bn, m // bm),
        in_specs=in_specs,
        out_specs=out_specs,
        out_shape=out_shapes,
        scratch_shapes=[pltpu.VMEM((k, n_w * bn), BF16)] if staged else [],
        compiler_params=_params("arbitrary", "arbitrary"),
        name=name,
    )(a, *[w for w, _ in weights], *extras)


def _matmul_plain(a, w, out_dtype):
    m, n = a.shape[0], w.shape[1]
    bm, bn = _tile(ROW_TILE, m), _tile(COL_TILE, n)

    def epilogue(accs, ex, outs, rows):
        outs[0][rows, :] = accs[0].astype(out_dtype)

    return _matmul_call(a, [(w, 0)], [], [], [jax.ShapeDtypeStruct((m, n), out_dtype)],
                        [pl.BlockSpec((bm, bn), lambda j, i: (i, j))], epilogue,
                        bm=bm, bn=bn, n_cols=n, name="matmul")[0]


def _matmul_gated_residual(geom, a, w, res, mod3, gate_kind, row_target=ROW_TILE):
    m, n = a.shape[0], w.shape[1]
    bm, bn = geom.row_tile(row_target), _tile(COL_TILE, n)

    def epilogue(accs, ex, outs, rows):
        res_ref, gate_ref = ex
        outs[0][rows, :] = res_ref[rows, :] + gate_ref[...] * accs[0]

    extra_specs = [pl.BlockSpec((bm, bn), lambda j, i: (i, j)),
                   pl.BlockSpec((None, 1, bn), lambda j, i: (geom.group_of_block(i, bm) * N_MOD + gate_kind, 0, j))]
    return _matmul_call(a, [(w, 0)], [res, mod3], extra_specs, [jax.ShapeDtypeStruct((m, n), F32)],
                        [pl.BlockSpec((bm, bn), lambda j, i: (i, j))], epilogue,
                        bm=bm, bn=bn, n_cols=n, name="matmul_gated_residual")[0]


def _matmul_swiglu(a, w13):
    m, f = a.shape[0], w13.shape[1] // 2
    bm, bn = _tile(ROW_TILE, m), _tile(SWIGLU_TILE, f)

    def epilogue(accs, ex, outs, rows):
        outs[0][rows, :] = (jax.nn.silu(accs[0]) * accs[1]).astype(BF16)

    return _matmul_call(a, [(w13, 0), (w13, f // bn)], [], [], [jax.ShapeDtypeStruct((m, f), BF16)],
                        [pl.BlockSpec((bm, bn), lambda j, i: (i, j))], epilogue,
                        bm=bm, bn=bn, n_cols=f, name="matmul_swiglu")[0]


def _matmul_heads(geom, a, w, col0, n_cols, gain_cols, cos_tab, sin_tab, *, head_width, n_valid, rb,
                  rope_slabs, bm, normed=True, row0=0, n_rows=None, with_f32=False):
    n_rows = a.shape[0] if n_rows is None else n_rows
    bn = _tile(COL_TILE, n_cols, col0)
    assert bn % head_width == 0 and head_width % LANES == 0 and row0 % bm == 0
    rb0 = row0 // bm

    def epilogue(accs, ex, outs, rows):
        acc = accs[0]
        gain_ref, cos_ref, sin_ref = ex
        for h in range(bn // head_width):
            lo = h * head_width
            y = acc[:, lo:lo + head_width]
            if normed:
                ms = jnp.sum(y * y, axis=-1, keepdims=True) / n_valid
                y = y * lax.rsqrt(ms + EPS) * gain_ref[:, lo:lo + head_width]
            for s in range(head_width // LANES):
                ys = y[:, s * LANES:(s + 1) * LANES]
                if normed and s in rope_slabs:
                    ys = _rope_slab(ys, cos_ref[rows, s * LANES:(s + 1) * LANES],
                                    sin_ref[rows, s * LANES:(s + 1) * LANES], rb)
                cols = slice(lo + s * LANES, lo + (s + 1) * LANES)
                outs[0][rows, cols] = ys.astype(BF16)
                if with_f32:
                    outs[1][rows, cols] = ys

    extra_specs = [pl.BlockSpec((1, bn), lambda j, i: (0, j)),
                   pl.BlockSpec((bm, head_width), lambda j, i: (geom.rope_block(rb0 + i, bm), 0)),
                   pl.BlockSpec((bm, head_width), lambda j, i: (geom.rope_block(rb0 + i, bm), 0))]
    out_shapes = [jax.ShapeDtypeStruct((n_rows, n_cols), BF16)]
    out_specs = [pl.BlockSpec((bm, bn), lambda j, i: (i, j))]
    if with_f32:
        out_shapes.append(jax.ShapeDtypeStruct((n_rows, n_cols), F32))
        out_specs.append(pl.BlockSpec((bm, bn), lambda j, i: (i, j)))
    return _matmul_call(a, [(w, col0 // bn)], [gain_cols, cos_tab, sin_tab], extra_specs,
                        out_shapes, out_specs, epilogue, bm=bm, bn=bn, n_cols=n_cols,
                        row_block0=rb0, n_rows=n_rows, n_sub=HEAD_SUB_BLOCKS, name="matmul_head_norm_rope")


def _matmul_mla_kv(ckv, w, kpe, cos_tab, sin_tab, kg_nope, kg_rope, rope_block, *, bm, rb, n_valid):
    m, n = ckv.shape[0], w.shape[1]
    hw = 2 * LANES
    bn = _tile(COL_TILE, n)
    assert bn % hw == 0

    def epilogue(accs, ex, outs, rows):
        acc = accs[0]
        kpe_ref, cos_ref, sin_ref, gn_ref, gr_ref = ex
        k_ref, v_ref = outs
        pe = kpe_ref[rows, :]
        ss_pe = jnp.sum(pe * pe, axis=-1, keepdims=True)
        pe_rot = _rope_slab(pe * gr_ref[...], cos_ref[rows, :], sin_ref[rows, :], rb)
        for h in range(bn // hw):
            kn = acc[:, h * hw:h * hw + LANES]
            ms = (jnp.sum(kn * kn, axis=-1, keepdims=True) + ss_pe) / n_valid
            r = lax.rsqrt(ms + EPS)
            k_ref[rows, h * hw:h * hw + LANES] = (kn * r * gn_ref[...]).astype(BF16)
            k_ref[rows, h * hw + LANES:(h + 1) * hw] = (pe_rot * r).astype(BF16)
            v_ref[rows, h * LANES:(h + 1) * LANES] = acc[:, h * hw + LANES:(h + 1) * hw].astype(BF16)

    extra_specs = [pl.BlockSpec((bm, LANES), lambda j, i: (i, 0)),
                   pl.BlockSpec((bm, LANES), lambda j, i: (rope_block(i), 0)),
                   pl.BlockSpec((bm, LANES), lambda j, i: (rope_block(i), 0)),
                   pl.BlockSpec((1, LANES), lambda j, i: (0, 0)),
                   pl.BlockSpec((1, LANES), lambda j, i: (0, 0))]
    return _matmul_call(ckv, [(w, 0)], [kpe, cos_tab, sin_tab, kg_nope, kg_rope], extra_specs,
                        [jax.ShapeDtypeStruct((m, n), BF16), jax.ShapeDtypeStruct((m, n // 2), BF16)],
                        [pl.BlockSpec((bm, bn), lambda j, i: (i, j)),
                         pl.BlockSpec((bm, bn // 2), lambda j, i: (i, j))], epilogue,
                        bm=bm, bn=bn, n_cols=n, name="matmul_mla_kv")


class _Softmax:
    N_SCRATCH = 6

    def __init__(self, s, p, m, l, alpha, acc):
        self.s, self.p, self.m, self.l, self.alpha, self.acc = s, p, m, l, alpha, acc

    @staticmethod
    def scratch(rows, tk, dv):
        assert tk % LANES == 0 and dv % LANES == 0
        return [pltpu.VMEM((rows, tk), F32), pltpu.VMEM((rows, tk), BF16),
                pltpu.VMEM((rows, LANES), F32), pltpu.VMEM((rows, LANES), F32), pltpu.VMEM((rows, LANES), F32),
                pltpu.VMEM((rows, dv), F32)]

    def start(self):
        self.m[...] = jnp.full(self.m.shape, NEG_BIG, F32)
        self.l[...] = jnp.zeros(self.l.shape, F32)
        self.acc[...] = jnp.zeros(self.acc.shape, F32)

    def scores(self, q, k):
        self.s[...] = lax.dot_general(q, k, (((1,), (1,)), ((), ())), preferred_element_type=F32)

    def update(self):
        rows, tk = self.s.shape
        rb = _tile(max(SUBLANES, 1 << ((SOFTMAX_BLOCK_ELEMS // tk).bit_length() - 1)), rows)
        blocks = [pl.ds(r * rb, rb) for r in range(rows // rb)]
        slabs = [pl.ds(j * LANES, LANES) for j in range(tk // LANES)]
        for sl in blocks:
            mx = self.s[sl, slabs[0]]
            for cols in slabs[1:]:
                mx = jnp.maximum(mx, self.s[sl, cols])
            m_old = self.m[sl, :]
            m_new = jnp.maximum(m_old, jnp.broadcast_to(jnp.max(mx, axis=1, keepdims=True), m_old.shape))
            self.alpha[sl, :] = jnp.exp2(m_old - m_new)
            self.m[sl, :] = m_new
        for sl in blocks:
            m = self.m[sl, :]
            total = None
            for cols in slabs:
                p = jnp.exp2((self.s[sl, cols] - m).astype(BF16))
                total = p if total is None else total + p
                self.p[sl, cols] = p
            self.l[sl, :] = self.alpha[sl, :] * self.l[sl, :] + total.astype(F32)

    def accumulate(self, v):
        pv = jnp.dot(self.p[...], v, preferred_element_type=F32)
        alpha = self.alpha[...]
        for j in range(self.acc.shape[1] // LANES):
            cols = pl.ds(j * LANES, LANES)
            self.acc[:, cols] = alpha * self.acc[:, cols] + pv[:, j * LANES:(j + 1) * LANES]

    @staticmethod
    def chunk(maps, queries, keys, v):
        for sm, q, k in zip(maps, queries, keys):
            sm.scores(q, k)
        for sm in maps:
            sm.update()
        for sm in maps:
            sm.accumulate(v)

    def result(self, lo, n):
        inv = 1.0 / jnp.sum(self.l[lo:lo + n, :], axis=1, keepdims=True)
        return self.acc[lo:lo + n, :] * inv


def _heads_per_step(n_rep, n_chunks, n_heads):
    return _tile(SHORT_SEQ_HEADS, n_heads) if n_rep == 1 and n_chunks == 1 else 1


def _attn_kernel(q_ref, k_ref, v_ref, o_ref, *scratch, n_rep, dk, dv, tk, hp):
    bq, t = q_ref.shape[0], k_ref.shape[0]
    sm = _Softmax(*scratch[:_Softmax.N_SCRATCH])
    if n_rep > 1:
        q_all = scratch[_Softmax.N_SCRATCH]
        for r in range(n_rep):
            q_all[r * bq:(r + 1) * bq, :] = q_ref[:, r * dk:(r + 1) * dk]
    for h in range(hp):
        sm.start()

        def body(c, carry, h=h):
            off = pl.multiple_of(c * tk, tk)
            q = q_all[...] if n_rep > 1 else q_ref[:, h * dk:(h + 1) * dk]
            _Softmax.chunk([sm], [q], [k_ref[pl.ds(off, tk), h * dk:(h + 1) * dk]],
                           v_ref[pl.ds(off, tk), h * dv:(h + 1) * dv])
            return carry

        lax.fori_loop(0, t // tk, body, 0)
        for r in range(n_rep):
            o_ref[:, (h * n_rep + r) * dv:(h * n_rep + r + 1) * dv] = sm.result(r * bq, bq).astype(o_ref.dtype)


def _into_existing(kernel, n_in, into):
    if into is None:
        return kernel, [], [], {}

    def aliased(*refs):
        return kernel(*refs[:n_in], *refs[n_in + 1:])

    return aliased, [pl.BlockSpec(memory_space=pl.ANY)], [into], {n_in: 0}


def _attention(q2d, k2d, v2d, *, n_batch, s, t, n_groups, n_rep, dk, dv, q_row0=0, kv_row0=0, into=None):
    bq = _tile(Q_ROWS_GROUPED if n_rep > 1 else Q_ROWS_SINGLE, s)
    tk = _kv_chunk(t)
    hp = _heads_per_step(n_rep, t // tk, n_groups)
    qw, ow = hp * n_rep * dk, hp * n_rep * dv
    assert q_row0 % bq == 0 and kv_row0 % t == 0
    qb0, kb0 = q_row0 // bq, kv_row0 // t
    nq = s // bq
    rows = n_rep * bq
    scratch = _Softmax.scratch(rows, tk, dv)
    if n_rep > 1:
        scratch.append(pltpu.VMEM((rows, dk), BF16))
    kernel = functools.partial(_attn_kernel, n_rep=n_rep, dk=dk, dv=dv, tk=tk, hp=hp)
    kernel, alias_specs, alias_args, aliases = _into_existing(kernel, 3, into)
    return pl.pallas_call(
        kernel,
        grid=(n_batch, n_groups // hp, nq),
        in_specs=[pl.BlockSpec((bq, qw), lambda b, g, i: (qb0 + b * nq + i, g)),
                  pl.BlockSpec((t, hp * dk), lambda b, g, i: (kb0 + b, g)),
                  pl.BlockSpec((t, hp * dv), lambda b, g, i: (kb0 + b, g))] + alias_specs,
        out_specs=pl.BlockSpec((bq, ow), lambda b, g, i: (qb0 + b * nq + i, g)),
        out_shape=jax.ShapeDtypeStruct((q2d.shape[0], n_groups * n_rep * dv), BF16),
        input_output_aliases=aliases,
        scratch_shapes=scratch,
        compiler_params=_params("arbitrary", "arbitrary", "arbitrary"),
        name="attention",
    )(q2d, k2d, v2d, *alias_args)


def _diff_attn_kernel(q_ref, k_ref, v_ref, lq1_ref, lk1_ref, lq2_ref, lk2_ref, g_ref, o_ref, *scratch,
                      d, tk, lam_init, hp):
    bq, t = q_ref.shape[0], k_ref.shape[0]
    w = 2 * d
    maps = (_Softmax(*scratch[:_Softmax.N_SCRATCH]), _Softmax(*scratch[_Softmax.N_SCRATCH:]))
    lam = (jnp.exp(jnp.sum(lq1_ref[...] * lk1_ref[...], keepdims=True))
           - jnp.exp(jnp.sum(lq2_ref[...] * lk2_ref[...], keepdims=True)) + lam_init)
    for h in range(hp):
        for sm in maps:
            sm.start()

        def body(c, carry, h=h):
            off = pl.multiple_of(c * tk, tk)
            queries = [q_ref[:, h * w + i * d:h * w + (i + 1) * d] for i in range(2)]
            keys = [k_ref[pl.ds(off, tk), h * w + i * d:h * w + (i + 1) * d] for i in range(2)]
            _Softmax.chunk(maps, queries, keys, v_ref[pl.ds(off, tk), h * w:(h + 1) * w])
            return carry

        lax.fori_loop(0, t // tk, body, 0)
        o = maps[0].result(0, bq) - lam * maps[1].result(0, bq)
        ms = jnp.mean(o * o, axis=-1, keepdims=True)
        o = o * lax.rsqrt(ms + EPS) * g_ref[...] * (1.0 - lam_init)
        o_ref[:, h * w:(h + 1) * w] = o.astype(o_ref.dtype)


def _diff_attention(q2d, k2d, v2d, lams, subln_g, *, n_batch, s, t, n_heads, d, lam_init, q_row0=0, kv_row0=0,
                    into=None):
    w = 2 * d
    bq = _tile(Q_ROWS_DIFF, s)
    tk = _kv_chunk(t)
    assert q_row0 % bq == 0 and kv_row0 % t == 0
    qb0, kb0 = q_row0 // bq, kv_row0 // t
    nq = s // bq
    hp = _heads_per_step(1, t // tk, n_heads)
    kernel = functools.partial(_diff_attn_kernel, d=d, tk=tk, lam_init=lam_init, hp=hp)
    kernel, alias_specs, alias_args, aliases = _into_existing(kernel, 8, into)
    vec = pl.BlockSpec((1, d), lambda b, h, i: (0, 0))
    return pl.pallas_call(
        kernel,
        grid=(n_batch, n_heads // hp, nq),
        in_specs=[pl.BlockSpec((bq, hp * w), lambda b, h, i: (qb0 + b * nq + i, h)),
                  pl.BlockSpec((t, hp * w), lambda b, h, i: (kb0 + b, h)),
                  pl.BlockSpec((t, hp * w), lambda b, h, i: (kb0 + b, h)),
                  vec, vec, vec, vec,
                  pl.BlockSpec((1, w), lambda b, h, i: (0, 0))] + alias_specs,
        out_specs=pl.BlockSpec((bq, hp * w), lambda b, h, i: (qb0 + b * nq + i, h)),
        out_shape=jax.ShapeDtypeStruct((q2d.shape[0], n_heads * w), BF16),
        input_output_aliases=aliases,
        scratch_shapes=_Softmax.scratch(bq, tk, w) + _Softmax.scratch(bq, tk, w),
        compiler_params=_params("arbitrary", "arbitrary", "arbitrary"),
        name="diff_attention",
    )(q2d, k2d, v2d, *[a.reshape(1, d) for a in lams], subln_g.reshape(1, w), *alias_args)


def _with_cache(cache2d, new2d, n_req, dec_seq):
    w = new2d.shape[-1]
    both = jnp.concatenate([cache2d.reshape(n_req, -1, w).astype(BF16), new2d.reshape(n_req, dec_seq, w)], axis=1)
    return both.reshape(-1, w)


def _qkv_attention_inputs(geom, dims, h, wqkv, qg, kg, q_cols, k_cols, v_cols, hd, cache):
    b, s, n_req, dec_seq = dims
    bm = geom.row_tile(ROW_TILE)
    cos_tab, sin_tab = _rope_table(dec_seq, hd, hd, 0, bm)
    heads = dict(head_width=hd, n_valid=hd, rb=hd // 4, rope_slabs=(0,), bm=bm)
    q_gain = (jnp.tile(qg, q_cols // hd) * (hd ** -0.5 * LOG2_E)).reshape(1, -1)
    k_gain = jnp.tile(kg, k_cols // hd).reshape(1, -1)
    v_gain = jnp.ones((1, v_cols), F32)
    ctx = dict(row0=0, n_rows=geom.n_ctx, with_f32=True)
    lat = dict(row0=geom.n_ctx, n_rows=n_req * dec_seq)
    k_args = (geom, h, wqkv, q_cols, k_cols, k_gain, cos_tab, sin_tab)
    v_args = (geom, h, wqkv, q_cols + k_cols, v_cols, v_gain, cos_tab, sin_tab)
    q, = _matmul_heads(geom, h, wqkv, 0, q_cols, q_gain, cos_tab, sin_tab, **heads)
    k, k_state = _matmul_heads(*k_args, **ctx, **heads)
    v, v_state = _matmul_heads(*v_args, normed=False, **ctx, **heads)
    k_lat, = _matmul_heads(*k_args, **lat, **heads)
    v_lat, = _matmul_heads(*v_args, normed=False, **lat, **heads)
    k_all = _with_cache(cache[0], k_lat, n_req, dec_seq)
    v_all = _with_cache(cache[1], v_lat, n_req, dec_seq)
    return q, k, v, k_all, v_all, k_state, v_state


def _gqa_mixer(geom, dims, h, p, cache):
    wqkv, qg, kg, _ = p
    b, s, n_req, dec_seq = dims
    d = wqkv.shape[0]
    hd = qg.shape[0]
    n_heads = d // hd
    n_kv = (wqkv.shape[1] - d) // (2 * hd)
    kv_w = n_kv * hd
    past = cache[0].shape[1]
    q, k, v, k_all, v_all, k_state, v_state = _qkv_attention_inputs(
        geom, dims, h, wqkv, qg, kg, d, kv_w, kv_w, hd, cache)
    common = dict(n_groups=n_kv, n_rep=n_heads // n_kv, dk=hd, dv=hd)
    o = _attention(q, k, v, n_batch=b, s=s, t=s, **common)
    o = _attention(q, k_all, v_all, n_batch=n_req, s=dec_seq, t=past + dec_seq, q_row0=geom.n_ctx, into=o, **common)
    state = (k_state.reshape(b, s, n_kv, hd), v_state.reshape(b, s, n_kv, hd))
    return o, state


def _diff_mixer(geom, dims, h, p, cache, lam_init):
    wqkv, qg, kg, lq1, lk1, lq2, lk2, subln_g, _ = p
    b, s, n_req, dec_seq = dims
    d = wqkv.shape[0]
    hd = qg.shape[0]
    n_heads = d // (2 * hd)
    past = cache[0].shape[1]
    q, k, v, k_all, v_all, k_state, v_state = _qkv_attention_inputs(
        geom, dims, h, wqkv, qg, kg, d, d, d, hd, cache)
    common = dict(n_heads=n_heads, d=hd, lam_init=lam_init)
    lams = (lq1, lk1, lq2, lk2)
    o = _diff_attention(q, k, v, lams, subln_g, n_batch=b, s=s, t=s, **common)
    o = _diff_attention(q, k_all, v_all, lams, subln_g, n_batch=n_req, s=dec_seq, t=past + dec_seq,
                        q_row0=geom.n_ctx, into=o, **common)
    state = (k_state.reshape(b, s, n_heads, 2, hd), v_state.reshape(b, s, n_heads, 2 * hd))
    return o, state


def _mla_mixer(geom, dims, h, p, cache):
    wdown, qa_g, wuq, kva_g, wukv, qg, kg, wo = p
    b, s, n_req, dec_seq = dims
    q_rank, kv_rank = wuq.shape[0], wukv.shape[0]
    rope_dim = wdown.shape[1] - q_rank - kv_rank
    qk_dim = qg.shape[0]
    nope = qk_dim - rope_dim
    v_dim = LANES
    n_heads = wo.shape[0] // v_dim
    assert nope == LANES and rope_dim <= LANES and wukv.shape[1] == n_heads * 2 * LANES
    assert q_rank % kv_rank == 0
    hw = 2 * LANES
    past = cache[0].shape[1]
    t = past + dec_seq
    n_ctx = geom.n_ctx

    n_down = wdown.shape[1]
    pad_cols = -n_down % COL_TILE
    down = _matmul_plain(h, jnp.pad(wdown, ((0, 0), (0, pad_cols))), F32)
    cq = _plain_norm(geom, down, 0, q_rank, qa_g, BF16)
    ckv = _plain_norm(geom, down, q_rank // kv_rank, kv_rank, kva_g, F32)
    kpe = down[:, q_rank + kv_rank:n_down]

    wuq_p = jnp.pad(wuq.reshape(q_rank, n_heads, qk_dim), ((0, 0), (0, 0), (0, hw - qk_dim))).reshape(q_rank, n_heads * hw)
    gq = (jnp.tile(jnp.pad(qg, (0, hw - qk_dim)), n_heads) * (qk_dim ** -0.5 * LOG2_E)).reshape(1, -1)
    bm = geom.row_tile(ROW_TILE)
    cos_q, sin_q = _rope_table(dec_seq, rope_dim, hw, nope, bm)
    q, = _matmul_heads(geom, cq, wuq_p, 0, n_heads * hw, gq, cos_q, sin_q, head_width=hw, n_valid=qk_dim,
                       rb=rope_dim // 4, rope_slabs=(1,), bm=bm)

    def rows(cached, new, width):
        lat = jnp.concatenate([cached.reshape(n_req, past, width), new[n_ctx:].reshape(n_req, dec_seq, width)], axis=1)
        return jnp.concatenate([lat.reshape(n_req * t, width), new[:n_ctx]], axis=0)

    ckv_all = rows(cache[0], ckv, kv_rank).astype(BF16)
    kpe_all = jnp.pad(rows(cache[1], kpe, rope_dim), ((0, 0), (0, LANES - rope_dim)))
    bmk = _tile(COL_TILE, past, dec_seq, n_ctx)
    per_req, past_blocks = t // bmk, past // bmk

    def rope_block(i):
        r = i % per_req
        return jnp.where((i < n_req * per_req) & (r >= past_blocks), 1 + r - past_blocks, 0)

    cos_k, sin_k = _rope_table(dec_seq, rope_dim, LANES, 0, bmk)
    kg_nope = kg[:nope].reshape(1, LANES)
    kg_rope = jnp.pad(kg[nope:], (0, LANES - rope_dim)).reshape(1, LANES)
    k_all, v_all = _matmul_mla_kv(ckv_all, wukv, kpe_all, cos_k, sin_k, kg_nope, kg_rope, rope_block,
                                  bm=bmk, rb=rope_dim // 4, n_valid=qk_dim)

    common = dict(n_groups=n_heads, n_rep=1, dk=hw, dv=v_dim)
    o = _attention(q, k_all, v_all, n_batch=b, s=s, t=s, kv_row0=n_req * t, **common)
    o = _attention(q, k_all, v_all, n_batch=n_req, s=dec_seq, t=t, q_row0=n_ctx, into=o, **common)
    state = (ckv[:n_ctx].reshape(b, s, kv_rank), kpe[:n_ctx].reshape(b, s, rope_dim))
    return o, state


def kernel(x_prompt, x_sample, c, cache_k0, cache_v0, cache_k1, cache_v1, cache_ckv2, cache_kpe2, cache_k3, cache_v3, c_ctx, ada_w0, ada_b0, norm1_g0, norm2_g0, ffn_w13_0, ffn_w2_0, gqa_wqkv0, gqa_qg0, gqa_kg0, gqa_wo0, ada_w1, ada_b1, norm1_g1, norm2_g1, ffn_w13_1, ffn_w2_1, diff_wqkv1, diff_qg1, diff_kg1, diff_lq1_1, diff_lk1_1, diff_lq2_1, diff_lk2_1, diff_subln_g1, diff_wo1, ada_w2, ada_b2, norm1_g2, norm2_g2, ffn_w13_2, ffn_w2_2, mla_wdown2, mla_qa_g2, mla_wuq2, mla_kva_g2, mla_wukv2, mla_qg2, mla_kg2, mla_wo2, ada_w3, ada_b3, norm1_g3, norm2_g3, ffn_w13_3, ffn_w2_3, gqa_wqkv3, gqa_qg3, gqa_kg3, gqa_wo3):
    layers = [
        ((ada_w0, ada_b0, norm1_g0, norm2_g0, ffn_w13_0, ffn_w2_0),
         (gqa_wqkv0, gqa_qg0, gqa_kg0, gqa_wo0), (cache_k0, cache_v0)),
        ((ada_w1, ada_b1, norm1_g1, norm2_g1, ffn_w13_1, ffn_w2_1),
         (diff_wqkv1, diff_qg1, diff_kg1, diff_lq1_1, diff_lk1_1, diff_lq2_1, diff_lk2_1, diff_subln_g1, diff_wo1),
         (cache_k1, cache_v1)),
        ((ada_w2, ada_b2, norm1_g2, norm2_g2, ffn_w13_2, ffn_w2_2),
         (mla_wdown2, mla_qa_g2, mla_wuq2, mla_kva_g2, mla_wukv2, mla_qg2, mla_kg2, mla_wo2),
         (cache_ckv2, cache_kpe2)),
        ((ada_w3, ada_b3, norm1_g3, norm2_g3, ffn_w13_3, ffn_w2_3),
         (gqa_wqkv3, gqa_qg3, gqa_kg3, gqa_wo3), (cache_k3, cache_v3)),
    ]
    b, s, d = x_prompt.shape
    n_req, dec_seq, _ = x_sample.shape
    assert dec_seq % GRID_W == 0
    geom = _Geom(b * s, n_req, dec_seq)
    dims = (b, s, n_req, dec_seq)
    n_groups = 1 + n_req
    assert n_groups <= SUBLANES

    x = jnp.concatenate([x_prompt.reshape(b * s, d), x_sample.reshape(n_req * dec_seq, d)], axis=0)
    cs = jnp.concatenate([c_ctx[None, :], c, jnp.zeros((SUBLANES - n_groups, d), F32)], axis=0)

    new_state = []
    for l, (common, mixer, cache) in enumerate(layers):
        ada_w, ada_b, g1, g2, w13, w2 = common
        mod = _modulation(cs, ada_w, ada_b)
        mod3 = mod[:n_groups].reshape(n_groups * N_MOD, 1, d)
        h = _norm_modulate(geom, x, 0, d, g1, mod3, 0, 1, BF16)
        kind = l % N_MIXERS
        if kind == 0:
            o, st = _gqa_mixer(geom, dims, h, mixer, cache)
        elif kind == 1:
            o, st = _diff_mixer(geom, dims, h, mixer, cache, 0.8 - 0.6 * math.exp(-0.3 * l))
        else:
            o, st = _mla_mixer(geom, dims, h, mixer, cache)
        new_state.extend(st)
        x = _matmul_gated_residual(geom, o, mixer[-1], x, mod3, 2)
        h = _norm_modulate(geom, x, 0, d, g2, mod3, 3, 4, BF16)
        hidden = _matmul_swiglu(h, w13)
        x = _matmul_gated_residual(geom, hidden, w2.astype(BF16), x, mod3, 5, row_target=ROW_TILE // 2)

    y_prompt = x[:b * s].reshape(b, s, d)
    y_sample = x[b * s:].reshape(n_req, dec_seq, d)
    return (y_prompt, y_sample, *new_state)
```

```python
import functools
import math

import jax
import jax.numpy as jnp
from jax import lax
from jax.experimental import pallas as pl
from jax.experimental.pallas import tpu as pltpu

F32 = jnp.float32
BF16 = jnp.bfloat16

EPS = 1e-6
ROPE_THETA = 10000.0
GRID_W = 64
N_MOD = 6
N_MIXERS = 3
LANES = 128
SUBLANES = 8
VMEM_LIMIT_BYTES = 56 * 1024 * 1024
NEG_BIG = -1e30
LOG2_E = math.log2(math.e)

ROW_TILE = 1024
COL_TILE = 512
SWIGLU_TILE = 256
SWIGLU_ROWS = 2048
SWIGLU_VMEM_LIMIT_BYTES = 60 * 1024 * 1024
NORM_ROWS = 256
MOD_COLS = 1024
KV_CHUNK = 512
Q_ROWS_GROUPED = 2048
Q_ROWS_SINGLE = 4096
Q_ROWS_DIFF = 2048
SOFTMAX_BLOCK_ELEMS = 16 * SUBLANES * LANES
HEAD_SUB_BLOCKS = 4
SHORT_SEQ_HEADS = 8


def _params(*sem, vmem_limit=VMEM_LIMIT_BYTES):
    return pltpu.CompilerParams(dimension_semantics=sem, vmem_limit_bytes=vmem_limit)


def _tile(target, n, *also):
    t = min(target, n)
    while any(d % t for d in (n,) + also):
        t //= 2
    assert t >= 1
    return t


def _kv_chunk(t):
    for cand in range(min(KV_CHUNK, t) // LANES * LANES, 0, -LANES):
        if t % cand == 0:
            return cand
    return t


class _Geom:
    def __init__(self, n_ctx, n_req, dec_seq):
        self.n_ctx, self.n_req, self.dec_seq = n_ctx, n_req, dec_seq
        self.m = n_ctx + n_req * dec_seq

    def row_tile(self, target):
        return _tile(target, self.n_ctx, self.dec_seq)

    def group_of_block(self, i, bm):
        nc, per = self.n_ctx // bm, self.dec_seq // bm
        return jnp.where(i < nc, 0, 1 + (i - nc) // per)

    def rope_block(self, i, bm):
        nc, per = self.n_ctx // bm, self.dec_seq // bm
        return jnp.where(i < nc, 0, 1 + (i - nc) % per)


def _rope_cos_sin(n_tokens, rot_dim):
    rows = n_tokens // GRID_W
    row_pos = jnp.repeat(jnp.arange(rows, dtype=F32), GRID_W)
    col_pos = jnp.tile(jnp.arange(GRID_W, dtype=F32), rows)
    half = rot_dim // 2
    inv = ROPE_THETA ** (-jnp.arange(0, half, 2, dtype=F32) / half)
    ang_r = row_pos[:, None] * inv
    ang_c = col_pos[:, None] * inv
    cos = jnp.concatenate([jnp.cos(ang_r), jnp.cos(ang_r), jnp.cos(ang_c), jnp.cos(ang_c)], axis=-1)
    sin = jnp.concatenate([-jnp.sin(ang_r), jnp.sin(ang_r), -jnp.sin(ang_c), jnp.sin(ang_c)], axis=-1)
    return cos, sin


def _rope_table(n_tokens, rot_dim, width, lane0, bm):
    cos, sin = _rope_cos_sin(n_tokens, rot_dim)
    cos_full = jnp.ones((n_tokens, width), F32).at[:, lane0:lane0 + rot_dim].set(cos)
    sin_full = jnp.zeros((n_tokens, width), F32).at[:, lane0:lane0 + rot_dim].set(sin)
    cos_tab = jnp.concatenate([jnp.ones((bm, width), F32), cos_full], axis=0)
    sin_tab = jnp.concatenate([jnp.zeros((bm, width), F32), sin_full], axis=0)
    return cos_tab, sin_tab


def _rope_slab(y, cos, sin, rb):
    lane = lax.broadcasted_iota(jnp.int32, y.shape, 1)
    first = ((lane // rb) % 2) == 0
    partner = jnp.where(first, pltpu.roll(y, LANES - rb, 1), pltpu.roll(y, rb, 1))
    return y * cos + partner * sin


def _mod_kernel(c_ref, w_ref, b_ref, o_ref):
    a = jax.nn.silu(c_ref[...]).astype(BF16)
    o_ref[...] = jnp.dot(a, w_ref[...].astype(BF16), preferred_element_type=F32) + b_ref[...]


def _modulation(cs, w, b):
    d, n = w.shape
    bn = _tile(MOD_COLS, n)
    return pl.pallas_call(
        _mod_kernel,
        grid=(n // bn,),
        in_specs=[pl.BlockSpec((SUBLANES, d), lambda j: (0, 0)),
                  pl.BlockSpec((d, bn), lambda j: (0, j)),
                  pl.BlockSpec((1, bn), lambda j: (0, j))],
        out_specs=pl.BlockSpec((SUBLANES, bn), lambda j: (0, j)),
        out_shape=jax.ShapeDtypeStruct((SUBLANES, n), F32),
        compiler_params=_params("arbitrary"),
        name="adaln_modulation",
    )(cs, w, b.reshape(1, n))


def _norm_kernel(x_ref, g_ref, sh_ref, sc_ref, o_ref):
    x = x_ref[...]
    ms = jnp.mean(x * x, axis=-1, keepdims=True)
    y = x * lax.rsqrt(ms + EPS) * g_ref[...]
    o_ref[...] = (y * (1.0 + sc_ref[...]) + sh_ref[...]).astype(o_ref.dtype)


def _norm_modulate(geom, x, col_block, width, g, mod3, shift_kind, scale_kind, out_dtype):
    m = x.shape[0]
    bm = geom.row_tile(NORM_ROWS)

    def mod_spec(kind):
        return pl.BlockSpec((None, 1, width), lambda i: (geom.group_of_block(i, bm) * N_MOD + kind, 0, 0))

    return pl.pallas_call(
        _norm_kernel,
        grid=(m // bm,),
        in_specs=[pl.BlockSpec((bm, width), lambda i: (i, col_block)),
                  pl.BlockSpec((1, width), lambda i: (0, 0)),
                  mod_spec(shift_kind), mod_spec(scale_kind)],
        out_specs=pl.BlockSpec((bm, width), lambda i: (i, 0)),
        out_shape=jax.ShapeDtypeStruct((m, width), out_dtype),
        compiler_params=_params("arbitrary"),
        name="rmsnorm_modulate",
    )(x, g.reshape(1, width), mod3, mod3)


def _plain_norm(geom, x, col_block, width, g, out_dtype):
    zeros = jnp.zeros((N_MOD * (1 + geom.n_req), 1, width), F32)
    return _norm_modulate(geom, x, col_block, width, g, zeros, 0, 1, out_dtype)


def _matmul_call(a, weights, extras, extra_specs, out_shapes, out_specs, epilogue, *, bm, bn, n_cols, name,
                 n_sub=1, row_block0=0, n_rows=None, vmem_limit=VMEM_LIMIT_BYTES):
    k = a.shape[1]
    m = a.shape[0] if n_rows is None else n_rows
    n_w, n_ex, n_out = len(weights), len(extras), len(out_shapes)
    staged = n_w > 1 or weights[0][0].dtype != BF16

    def kernel(*refs):
        a_ref = refs[0]
        w_refs = refs[1:1 + n_w]
        ex_refs = refs[1 + n_w:1 + n_w + n_ex]
        out_refs = refs[1 + n_w + n_ex:1 + n_w + n_ex + n_out]
        if staged:
            w_bf16 = refs[-1]

            @pl.when(pl.program_id(1) == 0)
            def _():
                for idx, w_ref in enumerate(w_refs):
                    w_bf16[:, idx * bn:(idx + 1) * bn] = w_ref[...].astype(BF16)

            w_val = w_bf16[...]
        else:
            w_val = w_refs[0][...]
        sub = bm // n_sub
        for r in range(n_sub):
            rows = pl.ds(r * sub, sub)
            acc = jnp.dot(a_ref[rows, :], w_val, preferred_element_type=F32)
            epilogue([acc[:, idx * bn:(idx + 1) * bn] for idx in range(n_w)], ex_refs, out_refs, rows)

    in_specs = [pl.BlockSpec((bm, k), lambda j, i: (row_block0 + i, 0))]
    for _, off in weights:
        in_specs.append(pl.BlockSpec((k, bn), lambda j, i, off=off: (0, off + j)))
    in_specs += list(extra_specs)
    return pl.pallas_call(
        kernel,
        grid=(n_cols // bn, m // bm),
        in_specs=in_specs,
        out_specs=out_specs,
        out_shape=out_shapes,
        scratch_shapes=[pltpu.VMEM((k, n_w * bn), BF16)] if staged else [],
        compiler_params=_params("arbitrary", "arbitrary", vmem_limit=vmem_limit),
        name=name,
    )(a, *[w for w, _ in weights], *extras)


def _matmul_plain(a, w, out_dtype):
    m, n = a.shape[0], w.shape[1]
    bm, bn = _tile(ROW_TILE, m), _tile(COL_TILE, n)

    def epilogue(accs, ex, outs, rows):
        outs[0][rows, :] = accs[0].astype(out_dtype)

    return _matmul_call(a, [(w, 0)], [], [], [jax.ShapeDtypeStruct((m, n), out_dtype)],
                        [pl.BlockSpec((bm, bn), lambda j, i: (i, j))], epilogue,
                        bm=bm, bn=bn, n_cols=n, name="matmul")[0]


def _matmul_gated_residual(geom, a, w, res, mod3, gate_kind, row_target=ROW_TILE, row0=0, n_rows=None):
    n = w.shape[1]
    n_rows = a.shape[0] if n_rows is None else n_rows
    bm, bn = geom.row_tile(row_target), _tile(COL_TILE, n)
    assert row0 % bm == 0
    rb0 = row0 // bm

    def epilogue(accs, ex, outs, rows):
        res_ref, gate_ref = ex
        outs[0][rows, :] = res_ref[rows, :] + gate_ref[...] * accs[0]

    extra_specs = [pl.BlockSpec((bm, bn), lambda j, i: (rb0 + i, j)),
                   pl.BlockSpec((None, 1, bn),
                                lambda j, i: (geom.group_of_block(rb0 + i, bm) * N_MOD + gate_kind, 0, j))]
    return _matmul_call(a, [(w, 0)], [res, mod3], extra_specs, [jax.ShapeDtypeStruct((n_rows, n), F32)],
                        [pl.BlockSpec((bm, bn), lambda j, i: (i, j))], epilogue,
                        bm=bm, bn=bn, n_cols=n, row_block0=rb0, n_rows=n_rows, name="matmul_gated_residual")[0]


def _matmul_swiglu(a, w13):
    m, f = a.shape[0], w13.shape[1] // 2
    bm, bn = _tile(SWIGLU_ROWS, m), _tile(SWIGLU_TILE, f)

    def epilogue(accs, ex, outs, rows):
        outs[0][rows, :] = (jax.nn.silu(accs[0]) * accs[1]).astype(BF16)

    return _matmul_call(a, [(w13, 0), (w13, f // bn)], [], [], [jax.ShapeDtypeStruct((m, f), BF16)],
                        [pl.BlockSpec((bm, bn), lambda j, i: (i, j))], epilogue,
                        bm=bm, bn=bn, n_cols=f, vmem_limit=SWIGLU_VMEM_LIMIT_BYTES, name="matmul_swiglu")[0]


def _matmul_heads(geom, a, w, col0, n_cols, gain_cols, cos_tab, sin_tab, *, head_width, n_valid, rb,
                  rope_slabs, bm, normed=True, row0=0, n_rows=None, with_f32=False):
    n_rows = a.shape[0] if n_rows is None else n_rows
    bn = _tile(COL_TILE, n_cols, col0)
    assert bn % head_width == 0 and head_width % LANES == 0 and row0 % bm == 0
    rb0 = row0 // bm

    def epilogue(accs, ex, outs, rows):
        acc = accs[0]
        gain_ref, cos_ref, sin_ref = ex
        for h in range(bn // head_width):
            lo = h * head_width
            y = acc[:, lo:lo + head_width]
            if normed:
                ms = jnp.sum(y * y, axis=-1, keepdims=True) / n_valid
                y = y * lax.rsqrt(ms + EPS) * gain_ref[:, lo:lo + head_width]
            for s in range(head_width // LANES):
                ys = y[:, s * LANES:(s + 1) * LANES]
                if normed and s in rope_slabs:
                    ys = _rope_slab(ys, cos_ref[rows, s * LANES:(s + 1) * LANES],
                                    sin_ref[rows, s * LANES:(s + 1) * LANES], rb)
                cols = slice(lo + s * LANES, lo + (s + 1) * LANES)
                outs[0][rows, cols] = ys.astype(BF16)
                if with_f32:
                    outs[1][rows, cols] = ys

    extra_specs = [pl.BlockSpec((1, bn), lambda j, i: (0, j)),
                   pl.BlockSpec((bm, head_width), lambda j, i: (geom.rope_block(rb0 + i, bm), 0)),
                   pl.BlockSpec((bm, head_width), lambda j, i: (geom.rope_block(rb0 + i, bm), 0))]
    out_shapes = [jax.ShapeDtypeStruct((n_rows, n_cols), BF16)]
    out_specs = [pl.BlockSpec((bm, bn), lambda j, i: (i, j))]
    if with_f32:
        out_shapes.append(jax.ShapeDtypeStruct((n_rows, n_cols), F32))
        out_specs.append(pl.BlockSpec((bm, bn), lambda j, i: (i, j)))
    return _matmul_call(a, [(w, col0 // bn)], [gain_cols, cos_tab, sin_tab], extra_specs,
                        out_shapes, out_specs, epilogue, bm=bm, bn=bn, n_cols=n_cols,
                        row_block0=rb0, n_rows=n_rows, n_sub=HEAD_SUB_BLOCKS, name="matmul_head_norm_rope")


def _matmul_mla_kv(ckv, w, kpe, cos_tab, sin_tab, kg_nope, kg_rope, rope_block, *, bm, rb, n_valid):
    m, n = ckv.shape[0], w.shape[1]
    hw = 2 * LANES
    bn = _tile(COL_TILE, n)
    assert bn % hw == 0

    def epilogue(accs, ex, outs, rows):
        acc = accs[0]
        kpe_ref, cos_ref, sin_ref, gn_ref, gr_ref = ex
        k_ref, v_ref = outs
        pe = kpe_ref[rows, :]
        ss_pe = jnp.sum(pe * pe, axis=-1, keepdims=True)
        pe_rot = _rope_slab(pe * gr_ref[...], cos_ref[rows, :], sin_ref[rows, :], rb)
        for h in range(bn // hw):
            kn = acc[:, h * hw:h * hw + LANES]
            ms = (jnp.sum(kn * kn, axis=-1, keepdims=True) + ss_pe) / n_valid
            r = lax.rsqrt(ms + EPS)
            k_ref[rows, h * hw:h * hw + LANES] = (kn * r * gn_ref[...]).astype(BF16)
            k_ref[rows, h * hw + LANES:(h + 1) * hw] = (pe_rot * r).astype(BF16)
            v_ref[rows, h * LANES:(h + 1) * LANES] = acc[:, h * hw + LANES:(h + 1) * hw].astype(BF16)

    extra_specs = [pl.BlockSpec((bm, LANES), lambda j, i: (i, 0)),
                   pl.BlockSpec((bm, LANES), lambda j, i: (rope_block(i), 0)),
                   pl.BlockSpec((bm, LANES), lambda j, i: (rope_block(i), 0)),
                   pl.BlockSpec((1, LANES), lambda j, i: (0, 0)),
                   pl.BlockSpec((1, LANES), lambda j, i: (0, 0))]
    return _matmul_call(ckv, [(w, 0)], [kpe, cos_tab, sin_tab, kg_nope, kg_rope], extra_specs,
                        [jax.ShapeDtypeStruct((m, n), BF16), jax.ShapeDtypeStruct((m, n // 2), BF16)],
                        [pl.BlockSpec((bm, bn), lambda j, i: (i, j)),
                         pl.BlockSpec((bm, bn // 2), lambda j, i: (i, j))], epilogue,
                        bm=bm, bn=bn, n_cols=n, name="matmul_mla_kv")


class _Softmax:
    N_SCRATCH = 6

    def __init__(self, s, p, m, l, alpha, acc):
        self.s, self.p, self.m, self.l, self.alpha, self.acc = s, p, m, l, alpha, acc

    @staticmethod
    def scratch(rows, tk, dv):
        assert tk % LANES == 0 and dv % LANES == 0
        return [pltpu.VMEM((rows, tk), F32), pltpu.VMEM((rows, tk), BF16),
                pltpu.VMEM((rows, LANES), F32), pltpu.VMEM((rows, LANES), F32), pltpu.VMEM((rows, LANES), F32),
                pltpu.VMEM((rows, dv), F32)]

    def start(self):
        self.m[...] = jnp.full(self.m.shape, NEG_BIG, F32)
        self.l[...] = jnp.zeros(self.l.shape, F32)
        self.acc[...] = jnp.zeros(self.acc.shape, F32)

    def scores(self, q, k):
        self.s[...] = lax.dot_general(q, k, (((1,), (1,)), ((), ())), preferred_element_type=F32)

    def update(self):
        rows, tk = self.s.shape
        rb = _tile(max(SUBLANES, 1 << ((SOFTMAX_BLOCK_ELEMS // tk).bit_length() - 1)), rows)
        blocks = [pl.ds(r * rb, rb) for r in range(rows // rb)]
        slabs = [pl.ds(j * LANES, LANES) for j in range(tk // LANES)]
        for sl in blocks:
            mx = self.s[sl, slabs[0]]
            for cols in slabs[1:]:
                mx = jnp.maximum(mx, self.s[sl, cols])
            m_old = self.m[sl, :]
            m_new = jnp.maximum(m_old, jnp.broadcast_to(jnp.max(mx, axis=1, keepdims=True), m_old.shape))
            self.alpha[sl, :] = jnp.exp2(m_old - m_new)
            self.m[sl, :] = m_new
        for sl in blocks:
            m = self.m[sl, :]
            total = None
            for cols in slabs:
                p = jnp.exp2((self.s[sl, cols] - m).astype(BF16))
                total = p if total is None else total + p
                self.p[sl, cols] = p
            self.l[sl, :] = self.alpha[sl, :] * self.l[sl, :] + total.astype(F32)

    def accumulate(self, v):
        pv = jnp.dot(self.p[...], v, preferred_element_type=F32)
        alpha = self.alpha[...]
        for j in range(self.acc.shape[1] // LANES):
            cols = pl.ds(j * LANES, LANES)
            self.acc[:, cols] = alpha * self.acc[:, cols] + pv[:, j * LANES:(j + 1) * LANES]

    @staticmethod
    def chunk(maps, queries, keys, v):
        for sm, q, k in zip(maps, queries, keys):
            sm.scores(q, k)
        for sm in maps:
            sm.update()
        for sm in maps:
            sm.accumulate(v)

    def result(self, lo, n):
        inv = 1.0 / jnp.sum(self.l[lo:lo + n, :], axis=1, keepdims=True)
        return self.acc[lo:lo + n, :] * inv


def _heads_per_step(n_rep, n_chunks, n_heads):
    return _tile(SHORT_SEQ_HEADS, n_heads) if n_rep == 1 and n_chunks == 1 else 1


def _attn_kernel(q_ref, k_ref, v_ref, o_ref, *scratch, n_rep, dk, dv, tk, hp):
    bq, t = q_ref.shape[0], k_ref.shape[0]
    sm = _Softmax(*scratch[:_Softmax.N_SCRATCH])
    if n_rep > 1:
        q_all = scratch[_Softmax.N_SCRATCH]
        for r in range(n_rep):
            q_all[r * bq:(r + 1) * bq, :] = q_ref[:, r * dk:(r + 1) * dk]
    for h in range(hp):
        sm.start()

        def body(c, carry, h=h):
            off = pl.multiple_of(c * tk, tk)
            q = q_all[...] if n_rep > 1 else q_ref[:, h * dk:(h + 1) * dk]
            _Softmax.chunk([sm], [q], [k_ref[pl.ds(off, tk), h * dk:(h + 1) * dk]],
                           v_ref[pl.ds(off, tk), h * dv:(h + 1) * dv])
            return carry

        lax.fori_loop(0, t // tk, body, 0)
        for r in range(n_rep):
            o_ref[:, (h * n_rep + r) * dv:(h * n_rep + r + 1) * dv] = sm.result(r * bq, bq).astype(o_ref.dtype)


def _into_existing(kernel, n_in, into):
    if into is None:
        return kernel, [], [], {}

    def aliased(*refs):
        return kernel(*refs[:n_in], *refs[n_in + 1:])

    return aliased, [pl.BlockSpec(memory_space=pl.ANY)], [into], {n_in: 0}


def _attention(q2d, k2d, v2d, *, n_batch, s, t, n_groups, n_rep, dk, dv, q_row0=0, kv_row0=0, into=None):
    bq = _tile(Q_ROWS_GROUPED if n_rep > 1 else Q_ROWS_SINGLE, s)
    tk = _kv_chunk(t)
    hp = _heads_per_step(n_rep, t // tk, n_groups)
    qw, ow = hp * n_rep * dk, hp * n_rep * dv
    assert q_row0 % bq == 0 and kv_row0 % t == 0
    qb0, kb0 = q_row0 // bq, kv_row0 // t
    nq = s // bq
    rows = n_rep * bq
    scratch = _Softmax.scratch(rows, tk, dv)
    if n_rep > 1:
        scratch.append(pltpu.VMEM((rows, dk), BF16))
    kernel = functools.partial(_attn_kernel, n_rep=n_rep, dk=dk, dv=dv, tk=tk, hp=hp)
    kernel, alias_specs, alias_args, aliases = _into_existing(kernel, 3, into)
    return pl.pallas_call(
        kernel,
        grid=(n_batch, n_groups // hp, nq),
        in_specs=[pl.BlockSpec((bq, qw), lambda b, g, i: (qb0 + b * nq + i, g)),
                  pl.BlockSpec((t, hp * dk), lambda b, g, i: (kb0 + b, g)),
                  pl.BlockSpec((t, hp * dv), lambda b, g, i: (kb0 + b, g))] + alias_specs,
        out_specs=pl.BlockSpec((bq, ow), lambda b, g, i: (qb0 + b * nq + i, g)),
        out_shape=jax.ShapeDtypeStruct((q2d.shape[0], n_groups * n_rep * dv), BF16),
        input_output_aliases=aliases,
        scratch_shapes=scratch,
        compiler_params=_params("arbitrary", "arbitrary", "arbitrary"),
        name="attention",
    )(q2d, k2d, v2d, *alias_args)


def _diff_attn_kernel(q_ref, k_ref, v_ref, lq1_ref, lk1_ref, lq2_ref, lk2_ref, g_ref, o_ref, *scratch,
                      d, tk, lam_init, hp):
    bq, t = q_ref.shape[0], k_ref.shape[0]
    w = 2 * d
    maps = (_Softmax(*scratch[:_Softmax.N_SCRATCH]), _Softmax(*scratch[_Softmax.N_SCRATCH:]))
    lam = (jnp.exp(jnp.sum(lq1_ref[...] * lk1_ref[...], keepdims=True))
           - jnp.exp(jnp.sum(lq2_ref[...] * lk2_ref[...], keepdims=True)) + lam_init)
    for h in range(hp):
        for sm in maps:
            sm.start()

        def body(c, carry, h=h):
            off = pl.multiple_of(c * tk, tk)
            queries = [q_ref[:, h * w + i * d:h * w + (i + 1) * d] for i in range(2)]
            keys = [k_ref[pl.ds(off, tk), h * w + i * d:h * w + (i + 1) * d] for i in range(2)]
            _Softmax.chunk(maps, queries, keys, v_ref[pl.ds(off, tk), h * w:(h + 1) * w])
            return carry

        lax.fori_loop(0, t // tk, body, 0)
        o = maps[0].result(0, bq) - lam * maps[1].result(0, bq)
        ms = jnp.mean(o * o, axis=-1, keepdims=True)
        o = o * lax.rsqrt(ms + EPS) * g_ref[...] * (1.0 - lam_init)
        o_ref[:, h * w:(h + 1) * w] = o.astype(o_ref.dtype)


def _diff_attention(q2d, k2d, v2d, lams, subln_g, *, n_batch, s, t, n_heads, d, lam_init, q_row0=0, kv_row0=0,
                    into=None):
    w = 2 * d
    bq = _tile(Q_ROWS_DIFF, s)
    tk = _kv_chunk(t)
    assert q_row0 % bq == 0 and kv_row0 % t == 0
    qb0, kb0 = q_row0 // bq, kv_row0 // t
    nq = s // bq
    hp = _heads_per_step(1, t // tk, n_heads)
    kernel = functools.partial(_diff_attn_kernel, d=d, tk=tk, lam_init=lam_init, hp=hp)
    kernel, alias_specs, alias_args, aliases = _into_existing(kernel, 8, into)
    vec = pl.BlockSpec((1, d), lambda b, h, i: (0, 0))
    return pl.pallas_call(
        kernel,
        grid=(n_batch, n_heads // hp, nq),
        in_specs=[pl.BlockSpec((bq, hp * w), lambda b, h, i: (qb0 + b * nq + i, h)),
                  pl.BlockSpec((t, hp * w), lambda b, h, i: (kb0 + b, h)),
                  pl.BlockSpec((t, hp * w), lambda b, h, i: (kb0 + b, h)),
                  vec, vec, vec, vec,
                  pl.BlockSpec((1, w), lambda b, h, i: (0, 0))] + alias_specs,
        out_specs=pl.BlockSpec((bq, hp * w), lambda b, h, i: (qb0 + b * nq + i, h)),
        out_shape=jax.ShapeDtypeStruct((q2d.shape[0], n_heads * w), BF16),
        input_output_aliases=aliases,
        scratch_shapes=_Softmax.scratch(bq, tk, w) + _Softmax.scratch(bq, tk, w),
        compiler_params=_params("arbitrary", "arbitrary", "arbitrary"),
        name="diff_attention",
    )(q2d, k2d, v2d, *[a.reshape(1, d) for a in lams], subln_g.reshape(1, w), *alias_args)


def _with_cache(cache2d, new2d, n_req, dec_seq):
    w = new2d.shape[-1]
    both = jnp.concatenate([cache2d.reshape(n_req, -1, w).astype(BF16), new2d.reshape(n_req, dec_seq, w)], axis=1)
    return both.reshape(-1, w)


def _qkv_attention_inputs(geom, dims, h, wqkv, qg, kg, q_cols, k_cols, v_cols, hd, cache):
    b, s, n_req, dec_seq = dims
    bm = geom.row_tile(ROW_TILE)
    cos_tab, sin_tab = _rope_table(dec_seq, hd, hd, 0, bm)
    heads = dict(head_width=hd, n_valid=hd, rb=hd // 4, rope_slabs=(0,), bm=bm)
    q_gain = (jnp.tile(qg, q_cols // hd) * (hd ** -0.5 * LOG2_E)).reshape(1, -1)
    k_gain = jnp.tile(kg, k_cols // hd).reshape(1, -1)
    v_gain = jnp.ones((1, v_cols), F32)
    ctx = dict(row0=0, n_rows=geom.n_ctx, with_f32=True)
    lat = dict(row0=geom.n_ctx, n_rows=n_req * dec_seq)
    k_args = (geom, h, wqkv, q_cols, k_cols, k_gain, cos_tab, sin_tab)
    v_args = (geom, h, wqkv, q_cols + k_cols, v_cols, v_gain, cos_tab, sin_tab)
    q, = _matmul_heads(geom, h, wqkv, 0, q_cols, q_gain, cos_tab, sin_tab, **heads)
    k, k_state = _matmul_heads(*k_args, **ctx, **heads)
    v, v_state = _matmul_heads(*v_args, normed=False, **ctx, **heads)
    k_lat, = _matmul_heads(*k_args, **lat, **heads)
    v_lat, = _matmul_heads(*v_args, normed=False, **lat, **heads)
    k_all = _with_cache(cache[0], k_lat, n_req, dec_seq)
    v_all = _with_cache(cache[1], v_lat, n_req, dec_seq)
    return q, k, v, k_all, v_all, k_state, v_state


def _gqa_mixer(geom, dims, h, p, cache):
    wqkv, qg, kg, _ = p
    b, s, n_req, dec_seq = dims
    d = wqkv.shape[0]
    hd = qg.shape[0]
    n_heads = d // hd
    n_kv = (wqkv.shape[1] - d) // (2 * hd)
    kv_w = n_kv * hd
    past = cache[0].shape[1]
    q, k, v, k_all, v_all, k_state, v_state = _qkv_attention_inputs(
        geom, dims, h, wqkv, qg, kg, d, kv_w, kv_w, hd, cache)
    common = dict(n_groups=n_kv, n_rep=n_heads // n_kv, dk=hd, dv=hd)
    o = _attention(q, k, v, n_batch=b, s=s, t=s, **common)
    o = _attention(q, k_all, v_all, n_batch=n_req, s=dec_seq, t=past + dec_seq, q_row0=geom.n_ctx, into=o, **common)
    state = (k_state.reshape(b, s, n_kv, hd), v_state.reshape(b, s, n_kv, hd))
    return o, state


def _diff_mixer(geom, dims, h, p, cache, lam_init):
    wqkv, qg, kg, lq1, lk1, lq2, lk2, subln_g, _ = p
    b, s, n_req, dec_seq = dims
    d = wqkv.shape[0]
    hd = qg.shape[0]
    n_heads = d // (2 * hd)
    past = cache[0].shape[1]
    q, k, v, k_all, v_all, k_state, v_state = _qkv_attention_inputs(
        geom, dims, h, wqkv, qg, kg, d, d, d, hd, cache)
    common = dict(n_heads=n_heads, d=hd, lam_init=lam_init)
    lams = (lq1, lk1, lq2, lk2)
    o = _diff_attention(q, k, v, lams, subln_g, n_batch=b, s=s, t=s, **common)
    o = _diff_attention(q, k_all, v_all, lams, subln_g, n_batch=n_req, s=dec_seq, t=past + dec_seq,
                        q_row0=geom.n_ctx, into=o, **common)
    state = (k_state.reshape(b, s, n_heads, 2, hd), v_state.reshape(b, s, n_heads, 2 * hd))
    return o, state


def _mla_mixer(geom, dims, h, p, cache):
    wdown, qa_g, wuq, kva_g, wukv, qg, kg, wo = p
    b, s, n_req, dec_seq = dims
    q_rank, kv_rank = wuq.shape[0], wukv.shape[0]
    rope_dim = wdown.shape[1] - q_rank - kv_rank
    qk_dim = qg.shape[0]
    nope = qk_dim - rope_dim
    v_dim = LANES
    n_heads = wo.shape[0] // v_dim
    assert nope == LANES and rope_dim <= LANES and wukv.shape[1] == n_heads * 2 * LANES
    assert q_rank % kv_rank == 0
    hw = 2 * LANES
    past = cache[0].shape[1]
    t = past + dec_seq
    n_ctx = geom.n_ctx

    n_down = wdown.shape[1]
    pad_cols = -n_down % COL_TILE
    down = _matmul_plain(h, jnp.pad(wdown, ((0, 0), (0, pad_cols))), F32)
    cq = _plain_norm(geom, down, 0, q_rank, qa_g, BF16)
    ckv = _plain_norm(geom, down, q_rank // kv_rank, kv_rank, kva_g, F32)
    kpe = down[:, q_rank + kv_rank:n_down]

    wuq_p = jnp.pad(wuq.reshape(q_rank, n_heads, qk_dim), ((0, 0), (0, 0), (0, hw - qk_dim))).reshape(q_rank, n_heads * hw)
    gq = (jnp.tile(jnp.pad(qg, (0, hw - qk_dim)), n_heads) * (qk_dim ** -0.5 * LOG2_E)).reshape(1, -1)
    bm = geom.row_tile(ROW_TILE)
    cos_q, sin_q = _rope_table(dec_seq, rope_dim, hw, nope, bm)
    q, = _matmul_heads(geom, cq, wuq_p, 0, n_heads * hw, gq, cos_q, sin_q, head_width=hw, n_valid=qk_dim,
                       rb=rope_dim // 4, rope_slabs=(1,), bm=bm)

    def rows(cached, new, width):
        lat = jnp.concatenate([cached.reshape(n_req, past, width), new[n_ctx:].reshape(n_req, dec_seq, width)], axis=1)
        return jnp.concatenate([lat.reshape(n_req * t, width), new[:n_ctx]], axis=0)

    ckv_all = rows(cache[0], ckv, kv_rank).astype(BF16)
    kpe_all = jnp.pad(rows(cache[1], kpe, rope_dim), ((0, 0), (0, LANES - rope_dim)))
    bmk = _tile(COL_TILE, past, dec_seq, n_ctx)
    per_req, past_blocks = t // bmk, past // bmk

    def rope_block(i):
        r = i % per_req
        return jnp.where((i < n_req * per_req) & (r >= past_blocks), 1 + r - past_blocks, 0)

    cos_k, sin_k = _rope_table(dec_seq, rope_dim, LANES, 0, bmk)
    kg_nope = kg[:nope].reshape(1, LANES)
    kg_rope = jnp.pad(kg[nope:], (0, LANES - rope_dim)).reshape(1, LANES)
    k_all, v_all = _matmul_mla_kv(ckv_all, wukv, kpe_all, cos_k, sin_k, kg_nope, kg_rope, rope_block,
                                  bm=bmk, rb=rope_dim // 4, n_valid=qk_dim)

    common = dict(n_groups=n_heads, n_rep=1, dk=hw, dv=v_dim)
    o = _attention(q, k_all, v_all, n_batch=b, s=s, t=s, kv_row0=n_req * t, **common)
    o = _attention(q, k_all, v_all, n_batch=n_req, s=dec_seq, t=t, q_row0=n_ctx, into=o, **common)
    state = (ckv[:n_ctx].reshape(b, s, kv_rank), kpe[:n_ctx].reshape(b, s, rope_dim))
    return o, state


def kernel(x_prompt, x_sample, c, cache_k0, cache_v0, cache_k1, cache_v1, cache_ckv2, cache_kpe2, cache_k3, cache_v3, c_ctx, ada_w0, ada_b0, norm1_g0, norm2_g0, ffn_w13_0, ffn_w2_0, gqa_wqkv0, gqa_qg0, gqa_kg0, gqa_wo0, ada_w1, ada_b1, norm1_g1, norm2_g1, ffn_w13_1, ffn_w2_1, diff_wqkv1, diff_qg1, diff_kg1, diff_lq1_1, diff_lk1_1, diff_lq2_1, diff_lk2_1, diff_subln_g1, diff_wo1, ada_w2, ada_b2, norm1_g2, norm2_g2, ffn_w13_2, ffn_w2_2, mla_wdown2, mla_qa_g2, mla_wuq2, mla_kva_g2, mla_wukv2, mla_qg2, mla_kg2, mla_wo2, ada_w3, ada_b3, norm1_g3, norm2_g3, ffn_w13_3, ffn_w2_3, gqa_wqkv3, gqa_qg3, gqa_kg3, gqa_wo3):
    layers = [
        ((ada_w0, ada_b0, norm1_g0, norm2_g0, ffn_w13_0, ffn_w2_0),
         (gqa_wqkv0, gqa_qg0, gqa_kg0, gqa_wo0), (cache_k0, cache_v0)),
        ((ada_w1, ada_b1, norm1_g1, norm2_g1, ffn_w13_1, ffn_w2_1),
         (diff_wqkv1, diff_qg1, diff_kg1, diff_lq1_1, diff_lk1_1, diff_lq2_1, diff_lk2_1, diff_subln_g1, diff_wo1),
         (cache_k1, cache_v1)),
        ((ada_w2, ada_b2, norm1_g2, norm2_g2, ffn_w13_2, ffn_w2_2),
         (mla_wdown2, mla_qa_g2, mla_wuq2, mla_kva_g2, mla_wukv2, mla_qg2, mla_kg2, mla_wo2),
         (cache_ckv2, cache_kpe2)),
        ((ada_w3, ada_b3, norm1_g3, norm2_g3, ffn_w13_3, ffn_w2_3),
         (gqa_wqkv3, gqa_qg3, gqa_kg3, gqa_wo3), (cache_k3, cache_v3)),
    ]
    b, s, d = x_prompt.shape
    n_req, dec_seq, _ = x_sample.shape
    assert dec_seq % GRID_W == 0
    geom = _Geom(b * s, n_req, dec_seq)
    dims = (b, s, n_req, dec_seq)
    n_groups = 1 + n_req
    assert n_groups <= SUBLANES

    x = jnp.concatenate([x_prompt.reshape(b * s, d), x_sample.reshape(n_req * dec_seq, d)], axis=0)
    cs = jnp.concatenate([c_ctx[None, :], c, jnp.zeros((SUBLANES - n_groups, d), F32)], axis=0)

    new_state = []
    for l, (common, mixer, cache) in enumerate(layers):
        ada_w, ada_b, g1, g2, w13, w2 = common
        mod = _modulation(cs, ada_w, ada_b)
        mod3 = mod[:n_groups].reshape(n_groups * N_MOD, 1, d)
        h = _norm_modulate(geom, x, 0, d, g1, mod3, 0, 1, BF16)
        kind = l % N_MIXERS
        if kind == 0:
            o, st = _gqa_mixer(geom, dims, h, mixer, cache)
        elif kind == 1:
            o, st = _diff_mixer(geom, dims, h, mixer, cache, 0.8 - 0.6 * math.exp(-0.3 * l))
        else:
            o, st = _mla_mixer(geom, dims, h, mixer, cache)
        new_state.extend(st)
        x = _matmul_gated_residual(geom, o, mixer[-1], x, mod3, 2)
        h = _norm_modulate(geom, x, 0, d, g2, mod3, 3, 4, BF16)
        hidden = _matmul_swiglu(h, w13)
        down = functools.partial(_matmul_gated_residual, geom, hidden, w2.astype(BF16), x, mod3, 5,
                                 row_target=ROW_TILE // 2)
        if l + 1 < len(layers):
            x = down()
        else:
            y_prompt = down(row0=0, n_rows=geom.n_ctx).reshape(b, s, d)
            y_sample = down(row0=geom.n_ctx, n_rows=n_req * dec_seq).reshape(n_req, dec_seq, d)
    return (y_prompt, y_sample, *new_state)
```

```python
import functools
import math

import jax
import jax.numpy as jnp
from jax import lax
from jax.experimental import pallas as pl
from jax.experimental.pallas import tpu as pltpu

F32 = jnp.float32
BF16 = jnp.bfloat16

EPS = 1e-6
ROPE_THETA = 10000.0
GRID_W = 64
N_MOD = 6
N_MIXERS = 3
LANES = 128
SUBLANES = 8
VMEM_LIMIT_BYTES = 56 * 1024 * 1024
NEG_BIG = -1e30
LOG2_E = math.log2(math.e)

ROW_TILE = 1024
COL_TILE = 512
SWIGLU_TILE = 256
SWIGLU_ROWS = 2048
SWIGLU_VMEM_LIMIT_BYTES = 60 * 1024 * 1024
NORM_ROWS = 256
MOD_COLS = 1024
KV_CHUNK = 512
Q_ROWS_GROUPED = 2048
Q_ROWS_SINGLE = 4096
Q_ROWS_DIFF = 2048
SOFTMAX_BLOCK_ELEMS = 16 * SUBLANES * LANES
HEAD_SUB_BLOCKS = 4
SHORT_SEQ_HEADS = 16


def _params(*sem, vmem_limit=VMEM_LIMIT_BYTES):
    return pltpu.CompilerParams(dimension_semantics=sem, vmem_limit_bytes=vmem_limit)


def _tile(target, n, *also):
    t = min(target, n)
    while any(d % t for d in (n,) + also):
        t //= 2
    assert t >= 1
    return t


def _kv_chunk(t):
    for cand in range(min(KV_CHUNK, t) // LANES * LANES, 0, -LANES):
        if t % cand == 0:
            return cand
    return t


class _Geom:
    def __init__(self, n_ctx, n_req, dec_seq):
        self.n_ctx, self.n_req, self.dec_seq = n_ctx, n_req, dec_seq
        self.m = n_ctx + n_req * dec_seq

    def row_tile(self, target):
        return _tile(target, self.n_ctx, self.dec_seq)

    def group_of_block(self, i, bm):
        nc, per = self.n_ctx // bm, self.dec_seq // bm
        return jnp.where(i < nc, 0, 1 + (i - nc) // per)

    def rope_block(self, i, bm):
        nc, per = self.n_ctx // bm, self.dec_seq // bm
        return jnp.where(i < nc, 0, 1 + (i - nc) % per)


def _rope_cos_sin(n_tokens, rot_dim):
    rows = n_tokens // GRID_W
    row_pos = jnp.repeat(jnp.arange(rows, dtype=F32), GRID_W)
    col_pos = jnp.tile(jnp.arange(GRID_W, dtype=F32), rows)
    half = rot_dim // 2
    inv = ROPE_THETA ** (-jnp.arange(0, half, 2, dtype=F32) / half)
    ang_r = row_pos[:, None] * inv
    ang_c = col_pos[:, None] * inv
    cos = jnp.concatenate([jnp.cos(ang_r), jnp.cos(ang_r), jnp.cos(ang_c), jnp.cos(ang_c)], axis=-1)
    sin = jnp.concatenate([-jnp.sin(ang_r), jnp.sin(ang_r), -jnp.sin(ang_c), jnp.sin(ang_c)], axis=-1)
    return cos, sin


def _rope_table(n_tokens, rot_dim, width, lane0, bm):
    cos, sin = _rope_cos_sin(n_tokens, rot_dim)
    cos_full = jnp.ones((n_tokens, width), F32).at[:, lane0:lane0 + rot_dim].set(cos)
    sin_full = jnp.zeros((n_tokens, width), F32).at[:, lane0:lane0 + rot_dim].set(sin)
    cos_tab = jnp.concatenate([jnp.ones((bm, width), F32), cos_full], axis=0)
    sin_tab = jnp.concatenate([jnp.zeros((bm, width), F32), sin_full], axis=0)
    return cos_tab, sin_tab


def _rope_slab(y, cos, sin, rb):
    lane = lax.broadcasted_iota(jnp.int32, y.shape, 1)
    first = ((lane // rb) % 2) == 0
    partner = jnp.where(first, pltpu.roll(y, LANES - rb, 1), pltpu.roll(y, rb, 1))
    return y * cos + partner * sin


def _mod_kernel(c_ref, w_ref, b_ref, o_ref):
    a = jax.nn.silu(c_ref[...]).astype(BF16)
    o_ref[...] = jnp.dot(a, w_ref[...].astype(BF16), preferred_element_type=F32) + b_ref[...]


def _modulation(cs, w, b):
    d, n = w.shape
    bn = _tile(MOD_COLS, n)
    return pl.pallas_call(
        _mod_kernel,
        grid=(n // bn,),
        in_specs=[pl.BlockSpec((SUBLANES, d), lambda j: (0, 0)),
                  pl.BlockSpec((d, bn), lambda j: (0, j)),
                  pl.BlockSpec((1, bn), lambda j: (0, j))],
        out_specs=pl.BlockSpec((SUBLANES, bn), lambda j: (0, j)),
        out_shape=jax.ShapeDtypeStruct((SUBLANES, n), F32),
        compiler_params=_params("arbitrary"),
        name="adaln_modulation",
    )(cs, w, b.reshape(1, n))


def _norm_kernel(x_ref, g_ref, sh_ref, sc_ref, o_ref):
    x = x_ref[...]
    ms = jnp.mean(x * x, axis=-1, keepdims=True)
    y = x * lax.rsqrt(ms + EPS) * g_ref[...]
    o_ref[...] = (y * (1.0 + sc_ref[...]) + sh_ref[...]).astype(o_ref.dtype)


def _norm_modulate(geom, x, col_block, width, g, mod3, shift_kind, scale_kind, out_dtype):
    m = x.shape[0]
    bm = geom.row_tile(NORM_ROWS)

    def mod_spec(kind):
        return pl.BlockSpec((None, 1, width), lambda i: (geom.group_of_block(i, bm) * N_MOD + kind, 0, 0))

    return pl.pallas_call(
        _norm_kernel,
        grid=(m // bm,),
        in_specs=[pl.BlockSpec((bm, width), lambda i: (i, col_block)),
                  pl.BlockSpec((1, width), lambda i: (0, 0)),
                  mod_spec(shift_kind), mod_spec(scale_kind)],
        out_specs=pl.BlockSpec((bm, width), lambda i: (i, 0)),
        out_shape=jax.ShapeDtypeStruct((m, width), out_dtype),
        compiler_params=_params("arbitrary"),
        name="rmsnorm_modulate",
    )(x, g.reshape(1, width), mod3, mod3)


def _plain_norm(geom, x, col_block, width, g, out_dtype):
    zeros = jnp.zeros((N_MOD * (1 + geom.n_req), 1, width), F32)
    return _norm_modulate(geom, x, col_block, width, g, zeros, 0, 1, out_dtype)


def _matmul_call(a, weights, extras, extra_specs, out_shapes, out_specs, epilogue, *, bm, bn, n_cols, name,
                 n_sub=1, row_block0=0, n_rows=None, vmem_limit=VMEM_LIMIT_BYTES):
    k = a.shape[1]
    m = a.shape[0] if n_rows is None else n_rows
    n_w, n_ex, n_out = len(weights), len(extras), len(out_shapes)
    staged = n_w > 1 or weights[0][0].dtype != BF16

    def kernel(*refs):
        a_ref = refs[0]
        w_refs = refs[1:1 + n_w]
        ex_refs = refs[1 + n_w:1 + n_w + n_ex]
        out_refs = refs[1 + n_w + n_ex:1 + n_w + n_ex + n_out]
        if staged:
            w_bf16 = refs[-1]

            @pl.when(pl.program_id(1) == 0)
            def _():
                for idx, w_ref in enumerate(w_refs):
                    w_bf16[:, idx * bn:(idx + 1) * bn] = w_ref[...].astype(BF16)

            w_val = w_bf16[...]
        else:
            w_val = w_refs[0][...]
        sub = bm // n_sub
        for r in range(n_sub):
            rows = pl.ds(r * sub, sub)
            acc = jnp.dot(a_ref[rows, :], w_val, preferred_element_type=F32)
            epilogue([acc[:, idx * bn:(idx + 1) * bn] for idx in range(n_w)], ex_refs, out_refs, rows)

    in_specs = [pl.BlockSpec((bm, k), lambda j, i: (row_block0 + i, 0))]
    for _, off in weights:
        in_specs.append(pl.BlockSpec((k, bn), lambda j, i, off=off: (0, off + j)))
    in_specs += list(extra_specs)
    return pl.pallas_call(
        kernel,
        grid=(n_cols // bn, m // bm),
        in_specs=in_specs,
        out_specs=out_specs,
        out_shape=out_shapes,
        scratch_shapes=[pltpu.VMEM((k, n_w * bn), BF16)] if staged else [],
        compiler_params=_params("arbitrary", "arbitrary", vmem_limit=vmem_limit),
        name=name,
    )(a, *[w for w, _ in weights], *extras)


def _matmul_plain(a, w, out_dtype):
    m, n = a.shape[0], w.shape[1]
    bm, bn = _tile(ROW_TILE, m), _tile(COL_TILE, n)

    def epilogue(accs, ex, outs, rows):
        outs[0][rows, :] = accs[0].astype(out_dtype)

    return _matmul_call(a, [(w, 0)], [], [], [jax.ShapeDtypeStruct((m, n), out_dtype)],
                        [pl.BlockSpec((bm, bn), lambda j, i: (i, j))], epilogue,
                        bm=bm, bn=bn, n_cols=n, name="matmul")[0]


def _matmul_gated_residual(geom, a, w, res, mod3, gate_kind, row_target=ROW_TILE, row0=0, n_rows=None):
    n = w.shape[1]
    n_rows = a.shape[0] if n_rows is None else n_rows
    bm, bn = geom.row_tile(row_target), _tile(COL_TILE, n)
    assert row0 % bm == 0
    rb0 = row0 // bm

    def epilogue(accs, ex, outs, rows):
        res_ref, gate_ref = ex
        outs[0][rows, :] = res_ref[rows, :] + gate_ref[...] * accs[0]

    extra_specs = [pl.BlockSpec((bm, bn), lambda j, i: (rb0 + i, j)),
                   pl.BlockSpec((None, 1, bn),
                                lambda j, i: (geom.group_of_block(rb0 + i, bm) * N_MOD + gate_kind, 0, j))]
    return _matmul_call(a, [(w, 0)], [res, mod3], extra_specs, [jax.ShapeDtypeStruct((n_rows, n), F32)],
                        [pl.BlockSpec((bm, bn), lambda j, i: (i, j))], epilogue,
                        bm=bm, bn=bn, n_cols=n, row_block0=rb0, n_rows=n_rows, name="matmul_gated_residual")[0]


def _matmul_swiglu(a, w13):
    m, f = a.shape[0], w13.shape[1] // 2
    bm, bn = _tile(SWIGLU_ROWS, m), _tile(SWIGLU_TILE, f)

    def epilogue(accs, ex, outs, rows):
        outs[0][rows, :] = (jax.nn.silu(accs[0]) * accs[1]).astype(BF16)

    return _matmul_call(a, [(w13, 0), (w13, f // bn)], [], [], [jax.ShapeDtypeStruct((m, f), BF16)],
                        [pl.BlockSpec((bm, bn), lambda j, i: (i, j))], epilogue,
                        bm=bm, bn=bn, n_cols=f, vmem_limit=SWIGLU_VMEM_LIMIT_BYTES, name="matmul_swiglu")[0]


def _matmul_heads(geom, a, w, col0, n_cols, gain_cols, cos_tab, sin_tab, *, head_width, n_valid, rb,
                  rope_slabs, bm, normed=True, row0=0, n_rows=None, with_f32=False):
    n_rows = a.shape[0] if n_rows is None else n_rows
    bn = _tile(COL_TILE, n_cols, col0)
    assert bn % head_width == 0 and head_width % LANES == 0 and row0 % bm == 0
    rb0 = row0 // bm

    def epilogue(accs, ex, outs, rows):
        acc = accs[0]
        gain_ref, cos_ref, sin_ref = ex
        for h in range(bn // head_width):
            lo = h * head_width
            y = acc[:, lo:lo + head_width]
            if normed:
                ms = jnp.sum(y * y, axis=-1, keepdims=True) / n_valid
                y = y * lax.rsqrt(ms + EPS) * gain_ref[:, lo:lo + head_width]
            for s in range(head_width // LANES):
                ys = y[:, s * LANES:(s + 1) * LANES]
                if normed and s in rope_slabs:
                    ys = _rope_slab(ys, cos_ref[rows, s * LANES:(s + 1) * LANES],
                                    sin_ref[rows, s * LANES:(s + 1) * LANES], rb)
                cols = slice(lo + s * LANES, lo + (s + 1) * LANES)
                outs[0][rows, cols] = ys.astype(BF16)
                if with_f32:
                    outs[1][rows, cols] = ys

    extra_specs = [pl.BlockSpec((1, bn), lambda j, i: (0, j)),
                   pl.BlockSpec((bm, head_width), lambda j, i: (geom.rope_block(rb0 + i, bm), 0)),
                   pl.BlockSpec((bm, head_width), lambda j, i: (geom.rope_block(rb0 + i, bm), 0))]
    out_shapes = [jax.ShapeDtypeStruct((n_rows, n_cols), BF16)]
    out_specs = [pl.BlockSpec((bm, bn), lambda j, i: (i, j))]
    if with_f32:
        out_shapes.append(jax.ShapeDtypeStruct((n_rows, n_cols), F32))
        out_specs.append(pl.BlockSpec((bm, bn), lambda j, i: (i, j)))
    return _matmul_call(a, [(w, col0 // bn)], [gain_cols, cos_tab, sin_tab], extra_specs,
                        out_shapes, out_specs, epilogue, bm=bm, bn=bn, n_cols=n_cols,
                        row_block0=rb0, n_rows=n_rows, n_sub=HEAD_SUB_BLOCKS, name="matmul_head_norm_rope")


def _matmul_mla_kv(ckv, w, kpe, cos_tab, sin_tab, kg_nope, kg_rope, rope_block, *, bm, rb, n_valid):
    m, n = ckv.shape[0], w.shape[1]
    hw = 2 * LANES
    bn = _tile(COL_TILE, n)
    assert bn % hw == 0

    def epilogue(accs, ex, outs, rows):
        acc = accs[0]
        kpe_ref, cos_ref, sin_ref, gn_ref, gr_ref = ex
        k_ref, v_ref = outs
        pe = kpe_ref[rows, :]
        ss_pe = jnp.sum(pe * pe, axis=-1, keepdims=True)
        pe_rot = _rope_slab(pe * gr_ref[...], cos_ref[rows, :], sin_ref[rows, :], rb)
        for h in range(bn // hw):
            kn = acc[:, h * hw:h * hw + LANES]
            ms = (jnp.sum(kn * kn, axis=-1, keepdims=True) + ss_pe) / n_valid
            r = lax.rsqrt(ms + EPS)
            k_ref[rows, h * hw:h * hw + LANES] = (kn * r * gn_ref[...]).astype(BF16)
            k_ref[rows, h * hw + LANES:(h + 1) * hw] = (pe_rot * r).astype(BF16)
            v_ref[rows, h * LANES:(h + 1) * LANES] = acc[:, h * hw + LANES:(h + 1) * hw].astype(BF16)

    extra_specs = [pl.BlockSpec((bm, LANES), lambda j, i: (i, 0)),
                   pl.BlockSpec((bm, LANES), lambda j, i: (rope_block(i), 0)),
                   pl.BlockSpec((bm, LANES), lambda j, i: (rope_block(i), 0)),
                   pl.BlockSpec((1, LANES), lambda j, i: (0, 0)),
                   pl.BlockSpec((1, LANES), lambda j, i: (0, 0))]
    return _matmul_call(ckv, [(w, 0)], [kpe, cos_tab, sin_tab, kg_nope, kg_rope], extra_specs,
                        [jax.ShapeDtypeStruct((m, n), BF16), jax.ShapeDtypeStruct((m, n // 2), BF16)],
                        [pl.BlockSpec((bm, bn), lambda j, i: (i, j)),
                         pl.BlockSpec((bm, bn // 2), lambda j, i: (i, j))], epilogue,
                        bm=bm, bn=bn, n_cols=n, name="matmul_mla_kv")


class _Softmax:
    N_SCRATCH = 6

    def __init__(self, s, p, m, l, alpha, acc):
        self.s, self.p, self.m, self.l, self.alpha, self.acc = s, p, m, l, alpha, acc

    @staticmethod
    def scratch(rows, tk, dv):
        assert tk % LANES == 0 and dv % LANES == 0
        return [pltpu.VMEM((rows, tk), F32), pltpu.VMEM((rows, tk), BF16),
                pltpu.VMEM((rows, LANES), F32), pltpu.VMEM((rows, LANES), F32), pltpu.VMEM((rows, LANES), F32),
                pltpu.VMEM((rows, dv), F32)]

    def start(self):
        self.m[...] = jnp.full(self.m.shape, NEG_BIG, F32)
        self.l[...] = jnp.zeros(self.l.shape, F32)
        self.acc[...] = jnp.zeros(self.acc.shape, F32)

    def scores(self, q, k):
        self.s[...] = lax.dot_general(q, k, (((1,), (1,)), ((), ())), preferred_element_type=F32)

    def update(self):
        rows, tk = self.s.shape
        rb = _tile(max(SUBLANES, 1 << ((SOFTMAX_BLOCK_ELEMS // tk).bit_length() - 1)), rows)
        blocks = [pl.ds(r * rb, rb) for r in range(rows // rb)]
        slabs = [pl.ds(j * LANES, LANES) for j in range(tk // LANES)]
        for sl in blocks:
            mx = self.s[sl, slabs[0]]
            for cols in slabs[1:]:
                mx = jnp.maximum(mx, self.s[sl, cols])
            m_old = self.m[sl, :]
            m_new = jnp.maximum(m_old, jnp.broadcast_to(jnp.max(mx, axis=1, keepdims=True), m_old.shape))
            self.alpha[sl, :] = jnp.exp2(m_old - m_new)
            self.m[sl, :] = m_new
        for sl in blocks:
            m = self.m[sl, :]
            total = None
            for cols in slabs:
                p = jnp.exp2((self.s[sl, cols] - m).astype(BF16))
                total = p if total is None else total + p
                self.p[sl, cols] = p
            self.l[sl, :] = self.alpha[sl, :] * self.l[sl, :] + total.astype(F32)

    def accumulate(self, v):
        pv = jnp.dot(self.p[...], v, preferred_element_type=F32)
        alpha = self.alpha[...]
        for j in range(self.acc.shape[1] // LANES):
            cols = pl.ds(j * LANES, LANES)
            self.acc[:, cols] = alpha * self.acc[:, cols] + pv[:, j * LANES:(j + 1) * LANES]

    @staticmethod
    def chunk(maps, queries, keys, v):
        for sm, q, k in zip(maps, queries, keys):
            sm.scores(q, k)
        for sm in maps:
            sm.update()
        for sm in maps:
            sm.accumulate(v)

    def result(self, lo, n):
        inv = 1.0 / jnp.sum(self.l[lo:lo + n, :], axis=1, keepdims=True)
        return self.acc[lo:lo + n, :] * inv


def _heads_per_step(n_rep, n_chunks, n_heads):
    return _tile(SHORT_SEQ_HEADS, n_heads) if n_rep == 1 and n_chunks == 1 else 1


def _attn_kernel(q_ref, k_ref, v_ref, o_ref, *scratch, n_rep, dk, dv, tk, hp):
    bq, t = q_ref.shape[0], k_ref.shape[0]
    sm = _Softmax(*scratch[:_Softmax.N_SCRATCH])
    if n_rep > 1:
        q_all = scratch[_Softmax.N_SCRATCH]
        for r in range(n_rep):
            q_all[r * bq:(r + 1) * bq, :] = q_ref[:, r * dk:(r + 1) * dk]
    for h in range(hp):
        sm.start()

        def body(c, carry, h=h):
            off = pl.multiple_of(c * tk, tk)
            q = q_all[...] if n_rep > 1 else q_ref[:, h * dk:(h + 1) * dk]
            _Softmax.chunk([sm], [q], [k_ref[pl.ds(off, tk), h * dk:(h + 1) * dk]],
                           v_ref[pl.ds(off, tk), h * dv:(h + 1) * dv])
            return carry

        lax.fori_loop(0, t // tk, body, 0)
        for r in range(n_rep):
            o_ref[:, (h * n_rep + r) * dv:(h * n_rep + r + 1) * dv] = sm.result(r * bq, bq).astype(o_ref.dtype)


def _into_existing(kernel, n_in, into):
    if into is None:
        return kernel, [], [], {}

    def aliased(*refs):
        return kernel(*refs[:n_in], *refs[n_in + 1:])

    return aliased, [pl.BlockSpec(memory_space=pl.ANY)], [into], {n_in: 0}


def _attention(q2d, k2d, v2d, *, n_batch, s, t, n_groups, n_rep, dk, dv, q_row0=0, kv_row0=0, into=None):
    bq = _tile(Q_ROWS_GROUPED if n_rep > 1 else Q_ROWS_SINGLE, s)
    tk = _kv_chunk(t)
    hp = _heads_per_step(n_rep, t // tk, n_groups)
    qw, ow = hp * n_rep * dk, hp * n_rep * dv
    assert q_row0 % bq == 0 and kv_row0 % t == 0
    qb0, kb0 = q_row0 // bq, kv_row0 // t
    nq = s // bq
    rows = n_rep * bq
    scratch = _Softmax.scratch(rows, tk, dv)
    if n_rep > 1:
        scratch.append(pltpu.VMEM((rows, dk), BF16))
    kernel = functools.partial(_attn_kernel, n_rep=n_rep, dk=dk, dv=dv, tk=tk, hp=hp)
    kernel, alias_specs, alias_args, aliases = _into_existing(kernel, 3, into)
    return pl.pallas_call(
        kernel,
        grid=(n_batch, n_groups // hp, nq),
        in_specs=[pl.BlockSpec((bq, qw), lambda b, g, i: (qb0 + b * nq + i, g)),
                  pl.BlockSpec((t, hp * dk), lambda b, g, i: (kb0 + b, g)),
                  pl.BlockSpec((t, hp * dv), lambda b, g, i: (kb0 + b, g))] + alias_specs,
        out_specs=pl.BlockSpec((bq, ow), lambda b, g, i: (qb0 + b * nq + i, g)),
        out_shape=jax.ShapeDtypeStruct((q2d.shape[0], n_groups * n_rep * dv), BF16),
        input_output_aliases=aliases,
        scratch_shapes=scratch,
        compiler_params=_params("arbitrary", "arbitrary", "arbitrary"),
        name="attention",
    )(q2d, k2d, v2d, *alias_args)


def _diff_attn_kernel(q_ref, k_ref, v_ref, lq1_ref, lk1_ref, lq2_ref, lk2_ref, g_ref, o_ref, *scratch,
                      d, tk, lam_init, hp):
    bq, t = q_ref.shape[0], k_ref.shape[0]
    w = 2 * d
    maps = (_Softmax(*scratch[:_Softmax.N_SCRATCH]), _Softmax(*scratch[_Softmax.N_SCRATCH:]))
    lam = (jnp.exp(jnp.sum(lq1_ref[...] * lk1_ref[...], keepdims=True))
           - jnp.exp(jnp.sum(lq2_ref[...] * lk2_ref[...], keepdims=True)) + lam_init)
    for h in range(hp):
        for sm in maps:
            sm.start()

        def body(c, carry, h=h):
            off = pl.multiple_of(c * tk, tk)
            queries = [q_ref[:, h * w + i * d:h * w + (i + 1) * d] for i in range(2)]
            keys = [k_ref[pl.ds(off, tk), h * w + i * d:h * w + (i + 1) * d] for i in range(2)]
            _Softmax.chunk(maps, queries, keys, v_ref[pl.ds(off, tk), h * w:(h + 1) * w])
            return carry

        lax.fori_loop(0, t // tk, body, 0)
        o = maps[0].result(0, bq) - lam * maps[1].result(0, bq)
        ms = jnp.mean(o * o, axis=-1, keepdims=True)
        o = o * lax.rsqrt(ms + EPS) * g_ref[...] * (1.0 - lam_init)
        o_ref[:, h * w:(h + 1) * w] = o.astype(o_ref.dtype)


def _diff_attention(q2d, k2d, v2d, lams, subln_g, *, n_batch, s, t, n_heads, d, lam_init, q_row0=0, kv_row0=0,
                    into=None):
    w = 2 * d
    bq = _tile(Q_ROWS_DIFF, s)
    tk = _kv_chunk(t)
    assert q_row0 % bq == 0 and kv_row0 % t == 0
    qb0, kb0 = q_row0 // bq, kv_row0 // t
    nq = s // bq
    hp = _heads_per_step(1, t // tk, n_heads)
    kernel = functools.partial(_diff_attn_kernel, d=d, tk=tk, lam_init=lam_init, hp=hp)
    kernel, alias_specs, alias_args, aliases = _into_existing(kernel, 8, into)
    vec = pl.BlockSpec((1, d), lambda b, h, i: (0, 0))
    return pl.pallas_call(
        kernel,
        grid=(n_batch, n_heads // hp, nq),
        in_specs=[pl.BlockSpec((bq, hp * w), lambda b, h, i: (qb0 + b * nq + i, h)),
                  pl.BlockSpec((t, hp * w), lambda b, h, i: (kb0 + b, h)),
                  pl.BlockSpec((t, hp * w), lambda b, h, i: (kb0 + b, h)),
                  vec, vec, vec, vec,
                  pl.BlockSpec((1, w), lambda b, h, i: (0, 0))] + alias_specs,
        out_specs=pl.BlockSpec((bq, hp * w), lambda b, h, i: (qb0 + b * nq + i, h)),
        out_shape=jax.ShapeDtypeStruct((q2d.shape[0], n_heads * w), BF16),
        input_output_aliases=aliases,
        scratch_shapes=_Softmax.scratch(bq, tk, w) + _Softmax.scratch(bq, tk, w),
        compiler_params=_params("arbitrary", "arbitrary", "arbitrary"),
        name="diff_attention",
    )(q2d, k2d, v2d, *[a.reshape(1, d) for a in lams], subln_g.reshape(1, w), *alias_args)


def _with_cache(cache2d, new2d, n_req, dec_seq):
    w = new2d.shape[-1]
    both = jnp.concatenate([cache2d.reshape(n_req, -1, w).astype(BF16), new2d.reshape(n_req, dec_seq, w)], axis=1)
    return both.reshape(-1, w)


def _qkv_attention_inputs(geom, dims, h, wqkv, qg, kg, q_cols, k_cols, v_cols, hd, cache):
    b, s, n_req, dec_seq = dims
    bm = geom.row_tile(ROW_TILE)
    cos_tab, sin_tab = _rope_table(dec_seq, hd, hd, 0, bm)
    heads = dict(head_width=hd, n_valid=hd, rb=hd // 4, rope_slabs=(0,), bm=bm)
    q_gain = (jnp.tile(qg, q_cols // hd) * (hd ** -0.5 * LOG2_E)).reshape(1, -1)
    k_gain = jnp.tile(kg, k_cols // hd).reshape(1, -1)
    v_gain = jnp.ones((1, v_cols), F32)
    ctx = dict(row0=0, n_rows=geom.n_ctx, with_f32=True)
    lat = dict(row0=geom.n_ctx, n_rows=n_req * dec_seq)
    k_args = (geom, h, wqkv, q_cols, k_cols, k_gain, cos_tab, sin_tab)
    v_args = (geom, h, wqkv, q_cols + k_cols, v_cols, v_gain, cos_tab, sin_tab)
    q, = _matmul_heads(geom, h, wqkv, 0, q_cols, q_gain, cos_tab, sin_tab, **heads)
    k, k_state = _matmul_heads(*k_args, **ctx, **heads)
    v, v_state = _matmul_heads(*v_args, normed=False, **ctx, **heads)
    k_lat, = _matmul_heads(*k_args, **lat, **heads)
    v_lat, = _matmul_heads(*v_args, normed=False, **lat, **heads)
    k_all = _with_cache(cache[0], k_lat, n_req, dec_seq)
    v_all = _with_cache(cache[1], v_lat, n_req, dec_seq)
    return q, k, v, k_all, v_all, k_state, v_state


def _gqa_mixer(geom, dims, h, p, cache):
    wqkv, qg, kg, _ = p
    b, s, n_req, dec_seq = dims
    d = wqkv.shape[0]
    hd = qg.shape[0]
    n_heads = d // hd
    n_kv = (wqkv.shape[1] - d) // (2 * hd)
    kv_w = n_kv * hd
    past = cache[0].shape[1]
    q, k, v, k_all, v_all, k_state, v_state = _qkv_attention_inputs(
        geom, dims, h, wqkv, qg, kg, d, kv_w, kv_w, hd, cache)
    common = dict(n_groups=n_kv, n_rep=n_heads // n_kv, dk=hd, dv=hd)
    o = _attention(q, k, v, n_batch=b, s=s, t=s, **common)
    o = _attention(q, k_all, v_all, n_batch=n_req, s=dec_seq, t=past + dec_seq, q_row0=geom.n_ctx, into=o, **common)
    state = (k_state.reshape(b, s, n_kv, hd), v_state.reshape(b, s, n_kv, hd))
    return o, state


def _diff_mixer(geom, dims, h, p, cache, lam_init):
    wqkv, qg, kg, lq1, lk1, lq2, lk2, subln_g, _ = p
    b, s, n_req, dec_seq = dims
    d = wqkv.shape[0]
    hd = qg.shape[0]
    n_heads = d // (2 * hd)
    past = cache[0].shape[1]
    q, k, v, k_all, v_all, k_state, v_state = _qkv_attention_inputs(
        geom, dims, h, wqkv, qg, kg, d, d, d, hd, cache)
    common = dict(n_heads=n_heads, d=hd, lam_init=lam_init)
    lams = (lq1, lk1, lq2, lk2)
    o = _diff_attention(q, k, v, lams, subln_g, n_batch=b, s=s, t=s, **common)
    o = _diff_attention(q, k_all, v_all, lams, subln_g, n_batch=n_req, s=dec_seq, t=past + dec_seq,
                        q_row0=geom.n_ctx, into=o, **common)
    state = (k_state.reshape(b, s, n_heads, 2, hd), v_state.reshape(b, s, n_heads, 2 * hd))
    return o, state


def _mla_mixer(geom, dims, h, p, cache):
    wdown, qa_g, wuq, kva_g, wukv, qg, kg, wo = p
    b, s, n_req, dec_seq = dims
    q_rank, kv_rank = wuq.shape[0], wukv.shape[0]
    rope_dim = wdown.shape[1] - q_rank - kv_rank
    qk_dim = qg.shape[0]
    nope = qk_dim - rope_dim
    v_dim = LANES
    n_heads = wo.shape[0] // v_dim
    assert nope == LANES and rope_dim <= LANES and wukv.shape[1] == n_heads * 2 * LANES
    assert q_rank % kv_rank == 0
    hw = 2 * LANES
    past = cache[0].shape[1]
    t = past + dec_seq
    n_ctx = geom.n_ctx

    n_down = wdown.shape[1]
    pad_cols = -n_down % COL_TILE
    down = _matmul_plain(h, jnp.pad(wdown, ((0, 0), (0, pad_cols))), F32)
    cq = _plain_norm(geom, down, 0, q_rank, qa_g, BF16)
    ckv = _plain_norm(geom, down, q_rank // kv_rank, kv_rank, kva_g, F32)
    kpe = down[:, q_rank + kv_rank:n_down]

    wuq_p = jnp.pad(wuq.reshape(q_rank, n_heads, qk_dim), ((0, 0), (0, 0), (0, hw - qk_dim))).reshape(q_rank, n_heads * hw)
    gq = (jnp.tile(jnp.pad(qg, (0, hw - qk_dim)), n_heads) * (qk_dim ** -0.5 * LOG2_E)).reshape(1, -1)
    bm = geom.row_tile(ROW_TILE)
    cos_q, sin_q = _rope_table(dec_seq, rope_dim, hw, nope, bm)
    q, = _matmul_heads(geom, cq, wuq_p, 0, n_heads * hw, gq, cos_q, sin_q, head_width=hw, n_valid=qk_dim,
                       rb=rope_dim // 4, rope_slabs=(1,), bm=bm)

    def rows(cached, new, width):
        lat = jnp.concatenate([cached.reshape(n_req, past, width), new[n_ctx:].reshape(n_req, dec_seq, width)], axis=1)
        return jnp.concatenate([lat.reshape(n_req * t, width), new[:n_ctx]], axis=0)

    ckv_all = rows(cache[0], ckv, kv_rank).astype(BF16)
    kpe_all = jnp.pad(rows(cache[1], kpe, rope_dim), ((0, 0), (0, LANES - rope_dim)))
    bmk = _tile(COL_TILE, past, dec_seq, n_ctx)
    per_req, past_blocks = t // bmk, past // bmk

    def rope_block(i):
        r = i % per_req
        return jnp.where((i < n_req * per_req) & (r >= past_blocks), 1 + r - past_blocks, 0)

    cos_k, sin_k = _rope_table(dec_seq, rope_dim, LANES, 0, bmk)
    kg_nope = kg[:nope].reshape(1, LANES)
    kg_rope = jnp.pad(kg[nope:], (0, LANES - rope_dim)).reshape(1, LANES)
    k_all, v_all = _matmul_mla_kv(ckv_all, wukv, kpe_all, cos_k, sin_k, kg_nope, kg_rope, rope_block,
                                  bm=bmk, rb=rope_dim // 4, n_valid=qk_dim)

    common = dict(n_groups=n_heads, n_rep=1, dk=hw, dv=v_dim)
    o = _attention(q, k_all, v_all, n_batch=b, s=s, t=s, kv_row0=n_req * t, **common)
    o = _attention(q, k_all, v_all, n_batch=n_req, s=dec_seq, t=t, q_row0=n_ctx, into=o, **common)
    state = (ckv[:n_ctx].reshape(b, s, kv_rank), kpe[:n_ctx].reshape(b, s, rope_dim))
    return o, state


def kernel(x_prompt, x_sample, c, cache_k0, cache_v0, cache_k1, cache_v1, cache_ckv2, cache_kpe2, cache_k3, cache_v3, c_ctx, ada_w0, ada_b0, norm1_g0, norm2_g0, ffn_w13_0, ffn_w2_0, gqa_wqkv0, gqa_qg0, gqa_kg0, gqa_wo0, ada_w1, ada_b1, norm1_g1, norm2_g1, ffn_w13_1, ffn_w2_1, diff_wqkv1, diff_qg1, diff_kg1, diff_lq1_1, diff_lk1_1, diff_lq2_1, diff_lk2_1, diff_subln_g1, diff_wo1, ada_w2, ada_b2, norm1_g2, norm2_g2, ffn_w13_2, ffn_w2_2, mla_wdown2, mla_qa_g2, mla_wuq2, mla_kva_g2, mla_wukv2, mla_qg2, mla_kg2, mla_wo2, ada_w3, ada_b3, norm1_g3, norm2_g3, ffn_w13_3, ffn_w2_3, gqa_wqkv3, gqa_qg3, gqa_kg3, gqa_wo3):
    layers = [
        ((ada_w0, ada_b0, norm1_g0, norm2_g0, ffn_w13_0, ffn_w2_0),
         (gqa_wqkv0, gqa_qg0, gqa_kg0, gqa_wo0), (cache_k0, cache_v0)),
        ((ada_w1, ada_b1, norm1_g1, norm2_g1, ffn_w13_1, ffn_w2_1),
         (diff_wqkv1, diff_qg1, diff_kg1, diff_lq1_1, diff_lk1_1, diff_lq2_1, diff_lk2_1, diff_subln_g1, diff_wo1),
         (cache_k1, cache_v1)),
        ((ada_w2, ada_b2, norm1_g2, norm2_g2, ffn_w13_2, ffn_w2_2),
         (mla_wdown2, mla_qa_g2, mla_wuq2, mla_kva_g2, mla_wukv2, mla_qg2, mla_kg2, mla_wo2),
         (cache_ckv2, cache_kpe2)),
        ((ada_w3, ada_b3, norm1_g3, norm2_g3, ffn_w13_3, ffn_w2_3),
         (gqa_wqkv3, gqa_qg3, gqa_kg3, gqa_wo3), (cache_k3, cache_v3)),
    ]
    b, s, d = x_prompt.shape
    n_req, dec_seq, _ = x_sample.shape
    assert dec_seq % GRID_W == 0
    geom = _Geom(b * s, n_req, dec_seq)
    dims = (b, s, n_req, dec_seq)
    n_groups = 1 + n_req
    assert n_groups <= SUBLANES

    x = jnp.concatenate([x_prompt.reshape(b * s, d), x_sample.reshape(n_req * dec_seq, d)], axis=0)
    cs = jnp.concatenate([c_ctx[None, :], c, jnp.zeros((SUBLANES - n_groups, d), F32)], axis=0)

    new_state = []
    for l, (common, mixer, cache) in enumerate(layers):
        ada_w, ada_b, g1, g2, w13, w2 = common
        mod = _modulation(cs, ada_w, ada_b)
        mod3 = mod[:n_groups].reshape(n_groups * N_MOD, 1, d)
        h = _norm_modulate(geom, x, 0, d, g1, mod3, 0, 1, BF16)
        kind = l % N_MIXERS
        if kind == 0:
            o, st = _gqa_mixer(geom, dims, h, mixer, cache)
        elif kind == 1:
            o, st = _diff_mixer(geom, dims, h, mixer, cache, 0.8 - 0.6 * math.exp(-0.3 * l))
        else:
            o, st = _mla_mixer(geom, dims, h, mixer, cache)
        new_state.extend(st)
        x = _matmul_gated_residual(geom, o, mixer[-1], x, mod3, 2)
        h = _norm_modulate(geom, x, 0, d, g2, mod3, 3, 4, BF16)
        hidden = _matmul_swiglu(h, w13)
        down = functools.partial(_matmul_gated_residual, geom, hidden, w2.astype(BF16), x, mod3, 5,
                                 row_target=ROW_TILE // 2)
        if l + 1 < len(layers):
            x = down()
        else:
            y_prompt = down(row0=0, n_rows=geom.n_ctx).reshape(b, s, d)
            y_sample = down(row0=geom.n_ctx, n_rows=n_req * dec_seq).reshape(n_req, dec_seq, d)
    return (y_prompt, y_sample, *new_state)
```
